```python
import math
import numpy as np
import jax
import jax.numpy as jnp
from jax import lax

D_MODEL = 1024
BATCH = 32
SEQ = 2048
DEPTH = 4

N_A_LAYERS = DEPTH // 2
N_B_LAYERS = DEPTH - N_A_LAYERS
DN_ALPHA = (2.0 * DEPTH) ** 0.25
DN_BETA = (8.0 * DEPTH) ** -0.25
LN_EPS = 1e-5
FFN_DIM = int(math.ceil(8 * D_MODEL / 3 / 256)) * 256
SGU_HIDDEN = 4 * D_MODEL
SGU_HALF = SGU_HIDDEN // 2
SGU_GROUPS = 8
SGU_GROUP_DIM = SGU_HALF // SGU_GROUPS
SGU_CHUNK = 128
N_HEADS = 16
N_KV_GROUPS = 4
HEADS_PER_GROUP = N_HEADS // N_KV_GROUPS
HEAD_DIM = D_MODEL // N_HEADS
CMP_LEN = 32
CMP_STRIDE = 16
PHI_HIDDEN = 256
SEL_LEN = 64
SEL_TOP_N = 8
N_LOCAL_BLOCKS = 2
WINDOW = 512
NSA_Q_BLOCK = 32
N_KV_SLOTS = 6
REL_BUCKETS = 32
REL_MAX_DIST = 128
NEG = -1e30

kernel_name = 'yoco_gmlp_nsa_macaron_deepnorm'


def layer_norm(x, g, b):
    xf = x.astype(jnp.float32)
    mu = jnp.mean(xf, axis=-1, keepdims=True)
    var = jnp.mean(jnp.square(xf - mu), axis=-1, keepdims=True)
    y = (xf - mu) * lax.rsqrt(var + LN_EPS)
    return (y * g.astype(jnp.float32) + b.astype(jnp.float32)).astype(x.dtype)


def post_norm(x, sub, g, b):
    return layer_norm(DN_ALPHA * x + sub, g, b)


def swiglu(x, w1, w3, w2):
    return (jax.nn.silu(x @ w1) * (x @ w3)) @ w2


def rel_bucket(dist):
    n = jnp.maximum(dist, 0)
    max_exact = REL_BUCKETS // 2
    nf = jnp.maximum(n, 1).astype(jnp.float32)
    large = max_exact + (jnp.log(nf / max_exact) / math.log(REL_MAX_DIST / max_exact)
                         * (REL_BUCKETS - max_exact)).astype(jnp.int32)
    return jnp.where(n < max_exact, n, jnp.minimum(large, REL_BUCKETS - 1))


def chunked_sgu_mixer(x, w_in, ln_g, ln_b, w_s, b_s, w_out):
    B, S, _ = x.shape
    z = jax.nn.gelu(x @ w_in, approximate=False)
    u, v = jnp.split(z, 2, axis=-1)
    v = layer_norm(v, ln_g, ln_b)
    n_chunks = S // SGU_CHUNK
    v = v.reshape(B, n_chunks, SGU_CHUNK, SGU_GROUPS, SGU_GROUP_DIM)
    causal = jnp.tril(jnp.ones((SGU_CHUNK, SGU_CHUNK), dtype=bool))
    w = jnp.where(causal[None], w_s, jnp.zeros_like(w_s))
    mixed = jnp.einsum('gts,bnsgc->bntgc', w, v) + b_s.T[None, None, :, :, None]
    return (u * mixed.reshape(B, S, SGU_HALF)) @ w_out


def compress(kv, pe, w1, b1, w2):
    B, S, G, Dh = kv.shape
    n_sub = CMP_LEN // CMP_STRIDE
    n_chunks = S // CMP_STRIDE
    n_cmp = n_chunks - n_sub + 1
    chunks = kv.reshape(B, n_chunks, CMP_STRIDE, G, Dh)
    blocks = jnp.concatenate([chunks[:, i:i + n_cmp] for i in range(n_sub)], axis=2)
    blocks = blocks + pe[None, None, :, None, :]
    flat = blocks.transpose(0, 1, 3, 2, 4).reshape(B, n_cmp, G, CMP_LEN * Dh)
    return jax.nn.gelu(flat @ w1 + b1, approximate=False) @ w2


def nsa_shared_kv(h, kv_w, cmp_pe, cmp_w1, cmp_b1, cmp_w2):
    B, S, _ = h.shape
    kv = (h @ kv_w).reshape(B, S, N_KV_SLOTS, N_KV_GROUPS, HEAD_DIM)
    k_cmp = compress(kv[:, :, 0], cmp_pe[0], cmp_w1[0], cmp_b1[0], cmp_w2[0])
    v_cmp = compress(kv[:, :, 1], cmp_pe[1], cmp_w1[1], cmp_b1[1], cmp_w2[1])
    n_sel = S // SEL_LEN
    k_sel = kv[:, :, 2].reshape(B, n_sel, SEL_LEN, N_KV_GROUPS, HEAD_DIM).transpose(0, 3, 1, 2, 4)
    v_sel = kv[:, :, 3].reshape(B, n_sel, SEL_LEN, N_KV_GROUPS, HEAD_DIM).transpose(0, 3, 1, 2, 4)
    pad = ((0, 0), (WINDOW, 0), (0, 0), (0, 0))
    k_win = jnp.pad(kv[:, :, 4], pad)
    v_win = jnp.pad(kv[:, :, 5], pad)
    return (k_cmp, v_cmp, k_sel, v_sel, k_win, v_win)


def masked_softmax(s, mask):
    p = jax.nn.softmax(jnp.where(mask, s, NEG), axis=-1)
    return jnp.where(mask, p, 0.0)


def cmp_to_sel_overlap(n_cmp, n_sel):
    i = np.arange(n_cmp)[:, None]
    j = np.arange(n_sel)[None, :]
    ov = (i * CMP_STRIDE < (j + 1) * SEL_LEN) & (i * CMP_STRIDE + CMP_LEN > j * SEL_LEN)
    return ov.astype(np.float32)


def nsa_mixer(h, shared, w_qg, w_o, rel_table):
    k_cmp, v_cmp, k_sel, v_sel, k_win, v_win = shared
    B, S, _ = h.shape
    G, HG, Dh, QB = N_KV_GROUPS, HEADS_PER_GROUP, HEAD_DIM, NSA_Q_BLOCK
    qg = h @ w_qg
    q = qg[..., :N_HEADS * Dh].reshape(B, S, G, HG, Dh) * (Dh ** -0.5)
    gates = jax.nn.sigmoid(qg[..., N_HEADS * Dh:]).reshape(B, S, G, HG, 3)
    n_qb = S // QB
    n_cmp = k_cmp.shape[1]
    n_sel = k_sel.shape[2]
    n_top = min(SEL_TOP_N, n_sel)
    overlap = jnp.asarray(cmp_to_sel_overlap(n_cmp, n_sel))
    cmp_end = jnp.arange(n_cmp, dtype=jnp.int32) * CMP_STRIDE + (CMP_LEN - 1)
    sel_ids = jnp.arange(n_sel, dtype=jnp.int32)
    b_idx = jnp.arange(B)[:, None, None]
    g_idx = jnp.arange(G)[None, :, None]
    g_idx4 = jnp.arange(G)[None, :, None, None]
    tbl_g = rel_table.reshape(REL_BUCKETS, G, HG)

    def head_bias(dist):
        return rel_table[rel_bucket(dist)].reshape(dist.shape + (G, HG)).transpose(2, 3, 0, 1).astype(jnp.float32)

    def block(args):
        q_b, g_b, q0 = args
        t = q0 + jnp.arange(QB, dtype=jnp.int32)
        dist_c = t[:, None] - cmp_end[None, :]
        mask_c = dist_c >= 0
        s_c = jnp.einsum('bqghd,bngd->bghqn', q_b, k_cmp).astype(jnp.float32) + head_bias(dist_c)
        p_c = masked_softmax(s_c, mask_c)
        o_c = jnp.einsum('bghqn,bngd->bqghd', p_c.astype(v_cmp.dtype), v_cmp)
        imp = jnp.einsum('bghqn,nj->bgqj', p_c, overlap)
        cur = t // SEL_LEN
        valid = sel_ids[None, :] <= cur[:, None]
        forced = valid & ((sel_ids[None, :] == 0) | (sel_ids[None, :] > cur[:, None] - N_LOCAL_BLOCKS))
        imp = jnp.where(forced, -NEG, jnp.where(valid, imp, NEG))
        _, idx = lax.top_k(imp, n_top)
        flat = idx.reshape(B, G, QB * n_top)
        k_g = k_sel[b_idx, g_idx, flat].reshape(B, G, QB, n_top * SEL_LEN, Dh)
        v_g = v_sel[b_idx, g_idx, flat].reshape(B, G, QB, n_top * SEL_LEN, Dh)
        pos = (idx[..., None] * SEL_LEN + jnp.arange(SEL_LEN, dtype=jnp.int32)).reshape(B, G, QB, n_top * SEL_LEN)
        dist_s = t[None, None, :, None] - pos
        mask_s = (dist_s >= 0)[:, :, None]
        bias_s = tbl_g[rel_bucket(dist_s), g_idx4].transpose(0, 1, 4, 2, 3).astype(jnp.float32)
        s_s = jnp.einsum('bqghd,bgqkd->bghqk', q_b, k_g).astype(jnp.float32) + bias_s
        p_s = masked_softmax(s_s, mask_s)
        o_s = jnp.einsum('bghqk,bgqkd->bqghd', p_s.astype(v_g.dtype), v_g)
        k_w = lax.dynamic_slice_in_dim(k_win, q0, WINDOW + QB, axis=1)
        v_w = lax.dynamic_slice_in_dim(v_win, q0, WINDOW + QB, axis=1)
        pos_w = q0 - WINDOW + jnp.arange(WINDOW + QB, dtype=jnp.int32)
        dist_w = t[:, None] - pos_w[None, :]
        mask_w = (dist_w >= 0) & (dist_w < WINDOW) & (pos_w[None, :] >= 0)
        s_w = jnp.einsum('bqghd,bkgd->bghqk', q_b, k_w).astype(jnp.float32) + head_bias(dist_w)
        p_w = masked_softmax(s_w, mask_w)
        o_w = jnp.einsum('bghqk,bkgd->bqghd', p_w.astype(v_w.dtype), v_w)
        return g_b[..., 0:1] * o_c + g_b[..., 1:2] * o_s + g_b[..., 2:3] * o_w

    q_blocks = q.reshape(B, n_qb, QB, G, HG, Dh).swapaxes(0, 1)
    g_blocks = gates.reshape(B, n_qb, QB, G, HG, 3).swapaxes(0, 1)
    starts = jnp.arange(n_qb, dtype=jnp.int32) * QB
    out = lax.map(block, (q_blocks, g_blocks, starts))
    out = out.swapaxes(0, 1).reshape(B, S, N_HEADS * Dh)
    return out @ w_o


def setup_inputs(seed: int = 0) -> dict:
    key = jax.random.key(seed)
    ks = jax.random.split(key, 24)
    f32 = jnp.float32

    def nrm(k, shape, scale):
        return jax.random.normal(k, shape, f32) * scale

    kv_scale = jnp.array([1.0, DN_BETA, 1.0, DN_BETA, 1.0, DN_BETA], f32)[None, :, None]
    kv_w = (nrm(ks[13], (D_MODEL, N_KV_SLOTS, N_KV_GROUPS * HEAD_DIM), D_MODEL ** -0.5) * kv_scale)
    return {
        'x': nrm(ks[0], (BATCH, SEQ, D_MODEL), 1.0),
        'rel_table': nrm(ks[1], (REL_BUCKETS, N_HEADS), 0.5),
        'ln_g': 1.0 + nrm(ks[2], (DEPTH, 3, D_MODEL), 0.05),
        'ln_b': nrm(ks[3], (DEPTH, 3, D_MODEL), 0.02),
        'ffn_w1': nrm(ks[4], (DEPTH, 2, D_MODEL, FFN_DIM), D_MODEL ** -0.5),
        'ffn_w3': nrm(ks[5], (DEPTH, 2, D_MODEL, FFN_DIM), D_MODEL ** -0.5),
        'ffn_w2': nrm(ks[6], (DEPTH, 2, FFN_DIM, D_MODEL), DN_BETA * FFN_DIM ** -0.5),
        'sgu_w_in': nrm(ks[7], (N_A_LAYERS, D_MODEL, SGU_HIDDEN), D_MODEL ** -0.5),
        'sgu_ln_g': 1.0 + nrm(ks[8], (N_A_LAYERS, SGU_HALF), 0.05),
        'sgu_ln_b': nrm(ks[9], (N_A_LAYERS, SGU_HALF), 0.02),
        'sgu_w_s': nrm(ks[10], (N_A_LAYERS, SGU_GROUPS, SGU_CHUNK, SGU_CHUNK), 0.05),
        'sgu_b_s': 1.0 + nrm(ks[11], (N_A_LAYERS, SGU_GROUPS, SGU_CHUNK), 0.05),
        'sgu_w_out': nrm(ks[12], (N_A_LAYERS, SGU_HALF, D_MODEL), DN_BETA * SGU_HALF ** -0.5),
        'kv_w': kv_w.reshape(D_MODEL, N_KV_SLOTS * N_KV_GROUPS * HEAD_DIM),
        'cmp_pe': nrm(ks[14], (2, CMP_LEN, HEAD_DIM), 0.5),
        'cmp_w1': nrm(ks[15], (2, CMP_LEN * HEAD_DIM, PHI_HIDDEN), (CMP_LEN * HEAD_DIM) ** -0.5),
        'cmp_b1': nrm(ks[16], (2, PHI_HIDDEN), 0.02),
        'cmp_w2': nrm(ks[17], (2, PHI_HIDDEN, HEAD_DIM), PHI_HIDDEN ** -0.5),
        'nsa_w_qg': nrm(ks[18], (N_B_LAYERS, D_MODEL, N_HEADS * HEAD_DIM + 3 * N_HEADS), D_MODEL ** -0.5),
        'nsa_w_o': nrm(ks[19], (N_B_LAYERS, N_HEADS * HEAD_DIM, D_MODEL), DN_BETA * (N_HEADS * HEAD_DIM) ** -0.5),
    }


def reference(x, rel_table, ln_g, ln_b, ffn_w1, ffn_w3, ffn_w2, sgu_w_in, sgu_ln_g, sgu_ln_b, sgu_w_s,
              sgu_b_s, sgu_w_out, kv_w, cmp_pe, cmp_w1, cmp_b1, cmp_w2, nsa_w_qg, nsa_w_o):
    h = x
    shared = None
    for layer in range(DEPTH):
        if layer == N_A_LAYERS:
            shared = nsa_shared_kv(h, kv_w, cmp_pe, cmp_w1, cmp_b1, cmp_w2)
        h = post_norm(h, 0.5 * swiglu(h, ffn_w1[layer, 0], ffn_w3[layer, 0], ffn_w2[layer, 0]),
                      ln_g[layer, 0], ln_b[layer, 0])
        if layer < N_A_LAYERS:
            a = layer
            mix = chunked_sgu_mixer(h, sgu_w_in[a], sgu_ln_g[a], sgu_ln_b[a], sgu_w_s[a], sgu_b_s[a], sgu_w_out[a])
        else:
            bl = layer - N_A_LAYERS
            mix = nsa_mixer(h, shared, nsa_w_qg[bl], nsa_w_o[bl], rel_table)
        h = post_norm(h, mix, ln_g[layer, 1], ln_b[layer, 1])
        h = post_norm(h, 0.5 * swiglu(h, ffn_w1[layer, 1], ffn_w3[layer, 1], ffn_w2[layer, 1]),
                      ln_g[layer, 2], ln_b[layer, 2])
    return h
```

```python
import functools
import math

import numpy as np
import jax
import jax.numpy as jnp
from jax import lax
from jax.experimental import pallas as pl
from jax.experimental.pallas import tpu as pltpu

DEPTH = 4
N_A_LAYERS = DEPTH // 2
DN_ALPHA = (2.0 * DEPTH) ** 0.25
LN_EPS = 1e-5
SGU_GROUPS = 8
SGU_CHUNK = 128
N_HEADS = 16
N_KV_GROUPS = 4
HEADS_PER_GROUP = N_HEADS // N_KV_GROUPS
HEAD_DIM = 64
CMP_LEN = 32
CMP_STRIDE = 16
SEL_LEN = 64
SEL_TOP_N = 8
N_LOCAL_BLOCKS = 2
WINDOW = 512
REL_BUCKETS = 32
REL_MAX_DIST = 128
NEG = -1e30
M_INIT = -1e20

BF16 = jnp.bfloat16
F32 = jnp.float32

FFN_ROWS = 512
FFN_COLS = 256
SGU_ROWS = 512
PROJ_ROWS = 512
ATT_Q = 256
GATE_COLS = 32
VMEM_LIMIT = 56 * 1024 * 1024


def _cparams(n_axes):
    return pltpu.CompilerParams(
        dimension_semantics=("arbitrary",) * n_axes, vmem_limit_bytes=VMEM_LIMIT)


def _const_spec(shape):
    nd = len(shape)
    return pl.BlockSpec(shape, lambda *_: (0,) * nd)


def _layer_norm(y, g, b):
    mu = jnp.mean(y, axis=-1, keepdims=True)
    yc = y - mu
    var = jnp.mean(yc * yc, axis=-1, keepdims=True)
    return yc * lax.rsqrt(var + LN_EPS) * g + b


def _gelu(x):
    return 0.5 * x * (1.0 + lax.erf(x * math.sqrt(0.5)))


def _dot(a, b):
    return jnp.dot(a, b, preferred_element_type=F32)


def _dot_nt(a, b):
    return lax.dot_general(a, b, (((1,), (1,)), ((), ())), preferred_element_type=F32)


def _ffn_body(h_ref, w1_ref, w3_ref, w2_ref, g_ref, b_ref, o_ref, hid_ref):
    x = h_ref[...]
    xb = x.astype(BF16)
    ffn = w1_ref.shape[1]
    for c in range(ffn // FFN_COLS):
        cs = slice(c * FFN_COLS, (c + 1) * FFN_COLS)
        a = _dot(xb, w1_ref[:, cs])
        b3 = _dot(xb, w3_ref[:, cs])
        hid_ref[:, cs] = (jax.nn.silu(a) * b3).astype(BF16)
    f = _dot(hid_ref[...], w2_ref[...])
    o_ref[...] = _layer_norm(DN_ALPHA * x + 0.5 * f, g_ref[...], b_ref[...])


def _ffn(h, w1, w3, w2, g, b):
    t, d = h.shape
    ffn = w1.shape[1]
    tm = min(FFN_ROWS, t)
    return pl.pallas_call(
        _ffn_body,
        grid=(t // tm,),
        in_specs=[
            pl.BlockSpec((tm, d), lambda i: (i, 0)),
            _const_spec((d, ffn)), _const_spec((d, ffn)), _const_spec((ffn, d)),
            _const_spec((1, d)), _const_spec((1, d)),
        ],
        out_specs=pl.BlockSpec((tm, d), lambda i: (i, 0)),
        out_shape=jax.ShapeDtypeStruct((t, d), F32),
        scratch_shapes=[pltpu.VMEM((tm, ffn), BF16)],
        compiler_params=_cparams(1),
        name="ffn",
    )(h, w1, w3, w2, g, b)


def _sgu_body(h_ref, win_ref, lg_ref, lb_ref, ws_ref, bs_ref, wout_ref, g_ref, b_ref,
              o_ref, v_ref, gat_ref):
    x = h_ref[...]
    xb = x.astype(BF16)
    tm = x.shape[0]
    half = wout_ref.shape[0]
    gd = half // SGU_GROUPS
    for c in range(SGU_GROUPS):
        cs = slice(c * gd, (c + 1) * gd)
        v_ref[:, cs] = _gelu(_dot(xb, win_ref[:, half + c * gd: half + (c + 1) * gd]))
    v = v_ref[...]
    mu = jnp.mean(v, axis=-1, keepdims=True)
    var = jnp.mean((v - mu) * (v - mu), axis=-1, keepdims=True)
    rstd = lax.rsqrt(var + LN_EPS)
    row = lax.broadcasted_iota(jnp.int32, (SGU_CHUNK, SGU_CHUNK), 0)
    col = lax.broadcasted_iota(jnp.int32, (SGU_CHUNK, SGU_CHUNK), 1)
    causal = col <= row
    for c in range(SGU_GROUPS):
        cs = slice(c * gd, (c + 1) * gd)
        vn = ((v_ref[:, cs] - mu) * rstd * lg_ref[:, cs] + lb_ref[:, cs]).astype(BF16)
        u = _gelu(_dot(xb, win_ref[:, cs]))
        w = jnp.where(causal, ws_ref[c], 0.0).astype(BF16)
        bias = bs_ref[:, c:c + 1]
        for r in range(tm // SGU_CHUNK):
            rs = slice(r * SGU_CHUNK, (r + 1) * SGU_CHUNK)
            mixed = _dot(w, vn[rs]) + bias
            gat_ref[rs, cs] = (u[rs] * mixed).astype(BF16)
    mix = _dot(gat_ref[...], wout_ref[...])
    o_ref[...] = _layer_norm(DN_ALPHA * x + mix, g_ref[...], b_ref[...])


def _sgu(h, w_in, ln_g, ln_b, w_s, b_s_t, w_out, g, b):
    t, d = h.shape
    hidden = w_in.shape[1]
    half = hidden // 2
    tm = min(SGU_ROWS, t)
    return pl.pallas_call(
        _sgu_body,
        grid=(t // tm,),
        in_specs=[
            pl.BlockSpec((tm, d), lambda i: (i, 0)),
            _const_spec((d, hidden)), _const_spec((1, half)), _const_spec((1, half)),
            _const_spec(w_s.shape), _const_spec(b_s_t.shape), _const_spec((half, d)),
            _const_spec((1, d)), _const_spec((1, d)),
        ],
        out_specs=pl.BlockSpec((tm, d), lambda i: (i, 0)),
        out_shape=jax.ShapeDtypeStruct((t, d), F32),
        scratch_shapes=[pltpu.VMEM((tm, half), F32), pltpu.VMEM((tm, half), BF16)],
        compiler_params=_cparams(1),
        name="sgu",
    )(h, w_in, ln_g, ln_b, w_s, b_s_t, w_out, g, b)


def _kv_body(h_ref, w_ref, cmp_ref, kv_ref):
    xb = h_ref[0].astype(BF16)
    n_cmp_cols = cmp_ref.shape[-1]
    cmp_ref[0] = _dot(xb, w_ref[:, :n_cmp_cols])
    gw = N_KV_GROUPS * HEAD_DIM
    for s in range(kv_ref.shape[0]):
        y = _dot(xb, w_ref[:, n_cmp_cols + s * gw: n_cmp_cols + (s + 1) * gw]).astype(BF16)
        for g in range(N_KV_GROUPS):
            kv_ref[s, 0, g] = y[:, g * HEAD_DIM:(g + 1) * HEAD_DIM]


def _kv_proj(h3, kv_w):
    bsz, seq, d = h3.shape
    gw = N_KV_GROUPS * HEAD_DIM
    ts = min(PROJ_ROWS, seq)
    return pl.pallas_call(
        _kv_body,
        grid=(bsz, seq // ts),
        in_specs=[pl.BlockSpec((1, ts, d), lambda b, i: (b, i, 0)), _const_spec(kv_w.shape)],
        out_specs=[
            pl.BlockSpec((1, ts, 2 * gw), lambda b, i: (b, i, 0)),
            pl.BlockSpec((4, 1, N_KV_GROUPS, ts, HEAD_DIM), lambda b, i: (0, b, 0, i, 0)),
        ],
        out_shape=[
            jax.ShapeDtypeStruct((bsz, seq, 2 * gw), F32),
            jax.ShapeDtypeStruct((4, bsz, N_KV_GROUPS, seq, HEAD_DIM), BF16),
        ],
        compiler_params=_cparams(2),
        name="kv_proj",
    )(h3, kv_w)


def _cmp_body(r_ref, pe_ref, w1_ref, b1_ref, w2_ref, o_ref):
    n_chunks = r_ref.shape[3]
    rows = N_KV_GROUPS * n_chunks
    half = r_ref.shape[4]
    r = r_ref[0, 0].reshape(rows, half)
    top = (r + pe_ref[0, :, :half]).astype(BF16)
    bot = (r + pe_ref[0, :, half:]).astype(BF16)
    a = _dot(top, w1_ref[0, :half])
    bm = _dot(bot, w1_ref[0, half:])
    pre = a + pltpu.roll(bm, rows - 1, 0) + b1_ref[0]
    hid = _gelu(pre).astype(BF16)
    o_ref[0, 0] = _dot(hid, w2_ref[0]).astype(BF16).reshape(N_KV_GROUPS, n_chunks, HEAD_DIM)


def _compress(r, pe, w1, b1, w2):
    _, bsz, g, n_chunks, width = r.shape
    phi = w1.shape[-1]
    return pl.pallas_call(
        _cmp_body,
        grid=(2, bsz),
        in_specs=[
            pl.BlockSpec((1, 1, g, n_chunks, width), lambda s, b: (s, b, 0, 0, 0)),
            pl.BlockSpec((1, 1, 2 * width), lambda s, b: (s, 0, 0)),
            pl.BlockSpec((1, 2 * width, phi), lambda s, b: (s, 0, 0)),
            pl.BlockSpec((1, 1, phi), lambda s, b: (s, 0, 0)),
            pl.BlockSpec((1, phi, HEAD_DIM), lambda s, b: (s, 0, 0)),
        ],
        out_specs=pl.BlockSpec((1, 1, g, n_chunks, HEAD_DIM), lambda s, b: (s, b, 0, 0, 0)),
        out_shape=jax.ShapeDtypeStruct((2, bsz, g, n_chunks, HEAD_DIM), BF16),
        compiler_params=_cparams(2),
        name="compress",
    )(r, pe, w1, b1, w2)


def _qg_body(h_ref, wq_ref, wg_ref, q_ref, gate_ref):
    xb = h_ref[0].astype(BF16)
    q = (_dot(xb, wq_ref[...]) * (HEAD_DIM ** -0.5)).astype(BF16)
    for hd in range(N_HEADS):
        q_ref[0, hd] = q[:, hd * HEAD_DIM:(hd + 1) * HEAD_DIM]
    gate = jax.nn.sigmoid(_dot(xb, wg_ref[...]))
    gcols = gate_ref.shape[-1]
    for g in range(N_KV_GROUPS):
        gate_ref[0, g] = gate[:, g * gcols:(g + 1) * gcols]


def _qg_proj(h3, wq, wg):
    bsz, seq, d = h3.shape
    ts = min(PROJ_ROWS, seq)
    gcols = wg.shape[1] // N_KV_GROUPS
    return pl.pallas_call(
        _qg_body,
        grid=(bsz, seq // ts),
        in_specs=[pl.BlockSpec((1, ts, d), lambda b, i: (b, i, 0)),
                  _const_spec(wq.shape), _const_spec(wg.shape)],
        out_specs=[
            pl.BlockSpec((1, N_HEADS, ts, HEAD_DIM), lambda b, i: (b, 0, i, 0)),
            pl.BlockSpec((1, N_KV_GROUPS, ts, gcols), lambda b, i: (b, 0, i, 0)),
        ],
        out_shape=[
            jax.ShapeDtypeStruct((bsz, N_HEADS, seq, HEAD_DIM), BF16),
            jax.ShapeDtypeStruct((bsz, N_KV_GROUPS, seq, gcols), F32),
        ],
        compiler_params=_cparams(2),
        name="qg_proj",
    )(h3, wq, wg)


def _attn_body(q_ref, kc_ref, vc_ref, ks_ref, vs_ref, kw_ref, vw_ref, bc_ref, bd_ref,
               gate_ref, ov_ref, ex_ref, o_ref,
               madd_ref, ms_ref, ls_ref, as_ref, mw_ref, lw_ref, aw_ref):
    tq = q_ref.shape[2]
    hg = q_ref.shape[1]
    n_sel = ov_ref.shape[0]
    qt = pl.program_id(2)
    q0 = qt * tq

    kc = kc_ref[0, 0]
    vc = vc_ref[0, 0]
    o_cmp = []
    psum = None
    for h in range(hg):
        s = _dot_nt(q_ref[0, h], kc) + bc_ref[0, h]
        m = jnp.maximum(jnp.max(s, axis=-1, keepdims=True), M_INIT)
        e = jnp.exp(s - m)
        l = jnp.sum(e, axis=-1, keepdims=True)
        p = e / jnp.where(l > 0.0, l, 1.0)
        o_cmp.append(_dot(p.astype(BF16), vc))
        psum = p if psum is None else psum + p

    p_hi = psum.astype(BF16)
    r1 = psum - p_hi.astype(F32)
    p_mid = r1.astype(BF16)
    p_lo = (r1 - p_mid.astype(F32)).astype(BF16)
    ov = ov_ref[...]
    imp = _dot_nt(ov, p_hi) + _dot_nt(ov, p_mid) + _dot_nt(ov, p_lo)
    blk = lax.broadcasted_iota(jnp.int32, (n_sel, tq), 0)
    cur = (q0 + lax.broadcasted_iota(jnp.int32, (n_sel, tq), 1)) // SEL_LEN
    valid = blk <= cur
    forced = jnp.logical_and(valid, jnp.logical_or(blk == 0, blk > cur - N_LOCAL_BLOCKS))
    val = jnp.where(forced, -NEG, jnp.where(valid, imp, NEG))
    rank = jnp.zeros((n_sel, tq), F32)
    for j in range(n_sel):
        rj = val[j:j + 1, :]
        ahead = jnp.logical_or(rj > val, jnp.logical_and(rj == val, blk > j))
        rank = rank + jnp.where(ahead, 1.0, 0.0)
    sel = jnp.where(rank < float(min(SEL_TOP_N, n_sel)), 1.0, 0.0)
    pad = ex_ref.shape[0] - n_sel
    if pad:
        sel = jnp.concatenate([sel, jnp.zeros((pad, tq), F32)], axis=0)
    keep = _dot(sel.T.astype(BF16), ex_ref[...])
    madd_ref[...] = (keep - 1.0) * (-NEG)

    def init(m_ref, l_ref, a_ref):
        m_ref[...] = jnp.full(m_ref.shape, M_INIT, F32)
        l_ref[...] = jnp.zeros(l_ref.shape, F32)
        a_ref[...] = jnp.zeros(a_ref.shape, F32)

    def tile(k_ref, v_ref, start, slot, m_ref, l_ref, a_ref, masked):
        start = pl.multiple_of(start, tq)
        k = k_ref[0, 0, pl.ds(start, tq), :]
        v = v_ref[0, 0, pl.ds(start, tq), :]
        extra = madd_ref[:, pl.ds(start, tq)] if masked else None
        for h in range(hg):
            s = _dot_nt(q_ref[0, h], k) + bd_ref[h, slot]
            if masked:
                s = s + extra
            m_old = m_ref[h]
            m_new = jnp.maximum(m_old, jnp.max(s, axis=-1, keepdims=True))
            alpha = jnp.exp(m_old - m_new)
            e = jnp.exp(s - m_new)
            l_ref[h] = alpha * l_ref[h] + jnp.sum(e, axis=-1, keepdims=True)
            a_ref[h] = alpha * a_ref[h] + _dot(e.astype(BF16), v)
            m_ref[h] = m_new

    init(ms_ref, ls_ref, as_ref)
    init(mw_ref, lw_ref, aw_ref)

    def sel_step(p, carry):
        delta = qt - p
        tile(ks_ref, vs_ref, p * tq, jnp.minimum(delta, 2), ms_ref, ls_ref, as_ref, True)
        return carry

    lax.fori_loop(0, qt + 1, sel_step, 0)

    def win_step(d, carry):
        tile(kw_ref, vw_ref, (qt - d) * tq, jnp.where(d == 2, 3, d), mw_ref, lw_ref, aw_ref, False)
        return carry

    lax.fori_loop(0, jnp.minimum(qt, WINDOW // tq) + 1, win_step, 0)

    gates = gate_ref[0, 0]
    for h in range(hg):
        g_c = gates[:, 3 * h:3 * h + 1]
        g_s = gates[:, 3 * h + 1:3 * h + 2]
        g_w = gates[:, 3 * h + 2:3 * h + 3]
        l_s = ls_ref[h]
        l_w = lw_ref[h]
        o_s = as_ref[h] / jnp.where(l_s > 0.0, l_s, 1.0)
        o_w = aw_ref[h] / jnp.where(l_w > 0.0, l_w, 1.0)
        out = g_c * o_cmp[h] + g_s * o_s + g_w * o_w
        o_ref[0, :, h * HEAD_DIM:(h + 1) * HEAD_DIM] = out.astype(BF16)


def _attention(q, kc, vc, kv4, bias_c, bias_d, gates, ov_t, expand):
    bsz, _, seq, _ = q.shape
    tq = min(ATT_Q, seq)
    hg = HEADS_PER_GROUP
    n_cmp = kc.shape[2]
    kv_spec = lambda slot: pl.BlockSpec(
        (1, 1, seq, HEAD_DIM), lambda b, g, i, slot=slot: (slot * bsz + b, g, 0, 0))
    kv_flat = kv4.reshape(4 * bsz, N_KV_GROUPS, seq, HEAD_DIM)
    cmp_spec = pl.BlockSpec((1, 1, n_cmp, HEAD_DIM), lambda b, g, i: (b, g, 0, 0))
    return pl.pallas_call(
        _attn_body,
        grid=(bsz, N_KV_GROUPS, seq // tq),
        in_specs=[
            pl.BlockSpec((1, hg, tq, HEAD_DIM), lambda b, g, i: (b, g, i, 0)),
            cmp_spec, cmp_spec,
            kv_spec(0), kv_spec(1), kv_spec(2), kv_spec(3),
            pl.BlockSpec((1, hg, tq, n_cmp), lambda b, g, i: (i, g, 0, 0)),
            pl.BlockSpec((hg, 4, tq, tq), lambda b, g, i: (g, 0, 0, 0)),
            pl.BlockSpec((1, 1, tq, gates.shape[-1]), lambda b, g, i: (b, g, i, 0)),
            _const_spec(ov_t.shape), _const_spec(expand.shape),
        ],
        out_specs=pl.BlockSpec((1, tq, hg * HEAD_DIM), lambda b, g, i: (b, i, g)),
        out_shape=jax.ShapeDtypeStruct((bsz, seq, N_HEADS * HEAD_DIM), BF16),
        scratch_shapes=[
            pltpu.VMEM((tq, seq), F32),
            pltpu.VMEM((hg, tq, 1), F32), pltpu.VMEM((hg, tq, 1), F32),
            pltpu.VMEM((hg, tq, HEAD_DIM), F32),
            pltpu.VMEM((hg, tq, 1), F32), pltpu.VMEM((hg, tq, 1), F32),
            pltpu.VMEM((hg, tq, HEAD_DIM), F32),
        ],
        compiler_params=_cparams(3),
        name="nsa_attn",
    )(q, kc, vc, kv_flat, kv_flat, kv_flat, kv_flat, bias_c, bias_d, gates, ov_t, expand)


def _oproj_body(h_ref, a_ref, w_ref, g_ref, b_ref, o_ref):
    mix = _dot(a_ref[...], w_ref[...])
    o_ref[...] = _layer_norm(DN_ALPHA * h_ref[...] + mix, g_ref[...], b_ref[...])


def _oproj(h, a, w_o, g, b):
    t, d = h.shape
    tm = min(PROJ_ROWS, t)
    return pl.pallas_call(
        _oproj_body,
        grid=(t // tm,),
        in_specs=[
            pl.BlockSpec((tm, d), lambda i: (i, 0)),
            pl.BlockSpec((tm, a.shape[1]), lambda i: (i, 0)),
            _const_spec(w_o.shape), _const_spec((1, d)), _const_spec((1, d)),
        ],
        out_specs=pl.BlockSpec((tm, d), lambda i: (i, 0)),
        out_shape=jax.ShapeDtypeStruct((t, d), F32),
        compiler_params=_cparams(1),
        name="oproj",
    )(h, a, w_o, g, b)


def _rel_bucket_np(dist):
    n = np.maximum(dist, 0)
    max_exact = REL_BUCKETS // 2
    nf = np.maximum(n, 1).astype(np.float32)
    large = max_exact + (np.log(nf / np.float32(max_exact)) / np.float32(math.log(REL_MAX_DIST / max_exact))
                         * np.float32(REL_BUCKETS - max_exact)).astype(np.int32)
    return np.where(n < max_exact, n, np.minimum(large, REL_BUCKETS - 1)).astype(np.int32)


def _bias_tables(rel_table, seq, tq, n_cmp_pad):
    q = np.arange(tq)[:, None]
    k = np.arange(tq)[None, :]
    far = REL_BUCKETS - 1
    d0 = q - k
    d1 = q - k + tq
    d3 = q - k + 2 * tq
    assert tq >= REL_MAX_DIST and 2 * tq == WINDOW
    buckets = np.stack([_rel_bucket_np(d0), _rel_bucket_np(d1),
                        np.full((tq, tq), far, np.int32), np.full((tq, tq), far, np.int32)])
    visible = np.stack([d0 >= 0, np.ones((tq, tq), bool), np.ones((tq, tq), bool), d3 < WINDOW])
    bias_d = jnp.where(visible[..., None], rel_table[buckets], NEG)
    bias_d = bias_d.transpose(3, 0, 1, 2)

    n_qt = seq // tq
    t = (np.arange(n_qt)[:, None, None] * tq + np.arange(tq)[None, :, None])
    i = np.arange(n_cmp_pad)[None, None, :]
    dc = t - (i * CMP_STRIDE + CMP_LEN - 1)
    n_cmp = seq // CMP_STRIDE - CMP_LEN // CMP_STRIDE + 1
    vis_c = np.logical_and(dc >= 0, i < n_cmp)
    bias_c = jnp.where(vis_c[..., None], rel_table[_rel_bucket_np(dc)], NEG)
    bias_c = bias_c.transpose(0, 3, 1, 2)
    return bias_c, bias_d


def _overlap_t(n_cmp_pad, n_sel):
    n_cmp = n_cmp_pad - CMP_LEN // CMP_STRIDE + 1
    i = np.arange(n_cmp_pad)[None, :]
    j = np.arange(n_sel)[:, None]
    ov = (i * CMP_STRIDE < (j + 1) * SEL_LEN) & (i * CMP_STRIDE + CMP_LEN > j * SEL_LEN) & (i < n_cmp)
    return ov.astype(np.float32)


def _expand(n_sel, seq):
    rows = max(128, n_sel)
    ex = np.zeros((rows, seq), np.float32)
    ex[np.arange(seq) // SEL_LEN, np.arange(seq)] = 1.0
    return ex


def kernel(x, rel_table, ln_g, ln_b, ffn_w1, ffn_w3, ffn_w2, sgu_w_in, sgu_ln_g, sgu_ln_b, sgu_w_s,
           sgu_b_s, sgu_w_out, kv_w, cmp_pe, cmp_w1, cmp_b1, cmp_w2, nsa_w_qg, nsa_w_o):
    bsz, seq, d = x.shape
    t = bsz * seq
    h = x.reshape(t, d)
    gw = N_KV_GROUPS * HEAD_DIM
    n_chunks = seq // CMP_STRIDE
    tq = min(ATT_Q, seq)

    def ffn(h, layer, half, norm):
        return _ffn(h, ffn_w1[layer, half].astype(BF16), ffn_w3[layer, half].astype(BF16),
                    ffn_w2[layer, half].astype(BF16), ln_g[layer, norm][None], ln_b[layer, norm][None])

    shared = None
    for layer in range(DEPTH):
        if layer == N_A_LAYERS:
            kv_cmp, kv4 = _kv_proj(h.reshape(bsz, seq, d), kv_w.astype(BF16))
            r = kv_cmp.reshape(bsz, n_chunks, CMP_STRIDE, 2, N_KV_GROUPS, HEAD_DIM)
            r = r.transpose(3, 0, 4, 1, 2, 5).reshape(2, bsz, N_KV_GROUPS, n_chunks, CMP_STRIDE * HEAD_DIM)
            kvc = _compress(r, cmp_pe.reshape(2, 1, CMP_LEN * HEAD_DIM), cmp_w1.astype(BF16),
                            cmp_b1[:, None, :], cmp_w2.astype(BF16))
            bias_c, bias_d = _bias_tables(rel_table, seq, tq, n_chunks)
            n_sel = seq // SEL_LEN
            shared = (kvc[0], kvc[1], kv4, bias_c, bias_d,
                      jnp.asarray(_overlap_t(n_chunks, n_sel), BF16), jnp.asarray(_expand(n_sel, seq), BF16))
        h = ffn(h, layer, 0, 0)
        if layer < N_A_LAYERS:
            a = layer
            h = _sgu(h, sgu_w_in[a].astype(BF16), sgu_ln_g[a][None], sgu_ln_b[a][None], sgu_w_s[a],
                     sgu_b_s[a].T, sgu_w_out[a].astype(BF16), ln_g[layer, 1][None], ln_b[layer, 1][None])
        else:
            bl = layer - N_A_LAYERS
            kc, vc, kv4, bias_c, bias_d, ov_t, expand = shared
            nq = N_HEADS * HEAD_DIM
            wq = nsa_w_qg[bl][:, :nq].astype(BF16)
            wg = nsa_w_qg[bl][:, nq:].reshape(d, N_KV_GROUPS, 3 * HEADS_PER_GROUP)
            wg = jnp.pad(wg, ((0, 0), (0, 0), (0, GATE_COLS - 3 * HEADS_PER_GROUP)))
            wg = wg.reshape(d, N_KV_GROUPS * GATE_COLS).astype(BF16)
            q, gates = _qg_proj(h.reshape(bsz, seq, d), wq, wg)
            att = _attention(q, kc, vc, kv4, bias_c, bias_d, gates, ov_t, expand)
            h = _oproj(h, att.reshape(t, nq), nsa_w_o[bl].astype(BF16), ln_g[layer, 1][None], ln_b[layer, 1][None])
        h = ffn(h, layer, 1, 2)
    return h.reshape(bsz, seq, d)
```

```python
import math

import numpy as np
import jax
import jax.numpy as jnp
from jax import lax
from jax.experimental import pallas as pl
from jax.experimental.pallas import tpu as pltpu

DEPTH = 4
N_A_LAYERS = DEPTH // 2
DN_ALPHA = (2.0 * DEPTH) ** 0.25
LN_EPS = 1e-5
SGU_GROUPS = 8
SGU_CHUNK = 128
N_HEADS = 16
N_KV_GROUPS = 4
HEADS_PER_GROUP = N_HEADS // N_KV_GROUPS
HEAD_DIM = 64
CMP_LEN = 32
CMP_STRIDE = 16
SEL_LEN = 64
SEL_TOP_N = 8
N_LOCAL_BLOCKS = 2
WINDOW = 512
REL_BUCKETS = 32
REL_MAX_DIST = 128
NEG = -1e30
M_INIT = -1e20
LOG2E = 1.0 / math.log(2.0)

BF16 = jnp.bfloat16
F32 = jnp.float32

FFN_ROWS = 512
FFN_COLS = 256
SGU_ROWS = 512
PROJ_ROWS = 512
ATT_Q = 256
ATT_ROWS = 128
LANES = 128
SEL_F0 = HEAD_DIM
BIAS_F0 = 96
MAX_SEL_BLOCKS = BIAS_F0 - SEL_F0
GATE_COLS = 32
VMEM_LIMIT = 56 * 1024 * 1024


def _cparams(n_axes):
    return pltpu.CompilerParams(
        dimension_semantics=("arbitrary",) * n_axes, vmem_limit_bytes=VMEM_LIMIT)


def _const_spec(shape):
    nd = len(shape)
    return pl.BlockSpec(shape, lambda *_: (0,) * nd)


def _layer_norm(y, g, b):
    mu = jnp.mean(y, axis=-1, keepdims=True)
    yc = y - mu
    var = jnp.mean(yc * yc, axis=-1, keepdims=True)
    return yc * lax.rsqrt(var + LN_EPS) * g + b


def _gelu(x):
    return 0.5 * x * (1.0 + lax.erf(x * math.sqrt(0.5)))


def _dot(a, b):
    return jnp.dot(a, b, preferred_element_type=F32)


def _dot_nt(a, b):
    return lax.dot_general(a, b, (((1,), (1,)), ((), ())), preferred_element_type=F32)


def _ffn_body(h_ref, w1_ref, w3_ref, w2_ref, g_ref, b_ref, o_ref, hid_ref):
    x = h_ref[...]
    xb = x.astype(BF16)
    ffn = w1_ref.shape[1]
    for c in range(ffn // FFN_COLS):
        cs = slice(c * FFN_COLS, (c + 1) * FFN_COLS)
        a = _dot(xb, w1_ref[:, cs])
        b3 = _dot(xb, w3_ref[:, cs])
        hid_ref[:, cs] = (jax.nn.silu(a) * b3).astype(BF16)
    f = _dot(hid_ref[...], w2_ref[...])
    o_ref[...] = _layer_norm(DN_ALPHA * x + 0.5 * f, g_ref[...], b_ref[...])


def _ffn(h, w1, w3, w2, g, b):
    t, d = h.shape
    ffn = w1.shape[1]
    tm = min(FFN_ROWS, t)
    return pl.pallas_call(
        _ffn_body,
        grid=(t // tm,),
        in_specs=[
            pl.BlockSpec((tm, d), lambda i: (i, 0)),
            _const_spec((d, ffn)), _const_spec((d, ffn)), _const_spec((ffn, d)),
            _const_spec((1, d)), _const_spec((1, d)),
        ],
        out_specs=pl.BlockSpec((tm, d), lambda i: (i, 0)),
        out_shape=jax.ShapeDtypeStruct((t, d), F32),
        scratch_shapes=[pltpu.VMEM((tm, ffn), BF16)],
        compiler_params=_cparams(1),
        name="ffn",
    )(h, w1, w3, w2, g, b)


def _sgu_body(h_ref, win_ref, lg_ref, lb_ref, ws_ref, bs_ref, wout_ref, g_ref, b_ref,
              o_ref, v_ref, gat_ref):
    x = h_ref[...]
    xb = x.astype(BF16)
    tm = x.shape[0]
    half = wout_ref.shape[0]
    gd = half // SGU_GROUPS
    for c in range(SGU_GROUPS):
        cs = slice(c * gd, (c + 1) * gd)
        v_ref[:, cs] = _gelu(_dot(xb, win_ref[:, half + c * gd: half + (c + 1) * gd]))
    v = v_ref[...]
    mu = jnp.mean(v, axis=-1, keepdims=True)
    var = jnp.mean((v - mu) * (v - mu), axis=-1, keepdims=True)
    rstd = lax.rsqrt(var + LN_EPS)
    row = lax.broadcasted_iota(jnp.int32, (SGU_CHUNK, SGU_CHUNK), 0)
    col = lax.broadcasted_iota(jnp.int32, (SGU_CHUNK, SGU_CHUNK), 1)
    causal = col <= row
    for c in range(SGU_GROUPS):
        cs = slice(c * gd, (c + 1) * gd)
        vn = ((v_ref[:, cs] - mu) * rstd * lg_ref[:, cs] + lb_ref[:, cs]).astype(BF16)
        u = _gelu(_dot(xb, win_ref[:, cs]))
        w = jnp.where(causal, ws_ref[c], 0.0).astype(BF16)
        bias = bs_ref[:, c:c + 1]
        for r in range(tm // SGU_CHUNK):
            rs = slice(r * SGU_CHUNK, (r + 1) * SGU_CHUNK)
            mixed = _dot(w, vn[rs]) + bias
            gat_ref[rs, cs] = (u[rs] * mixed).astype(BF16)
    mix = _dot(gat_ref[...], wout_ref[...])
    o_ref[...] = _layer_norm(DN_ALPHA * x + mix, g_ref[...], b_ref[...])


def _sgu(h, w_in, ln_g, ln_b, w_s, b_s_t, w_out, g, b):
    t, d = h.shape
    hidden = w_in.shape[1]
    half = hidden // 2
    tm = min(SGU_ROWS, t)
    return pl.pallas_call(
        _sgu_body,
        grid=(t // tm,),
        in_specs=[
            pl.BlockSpec((tm, d), lambda i: (i, 0)),
            _const_spec((d, hidden)), _const_spec((1, half)), _const_spec((1, half)),
            _const_spec(w_s.shape), _const_spec(b_s_t.shape), _const_spec((half, d)),
            _const_spec((1, d)), _const_spec((1, d)),
        ],
        out_specs=pl.BlockSpec((tm, d), lambda i: (i, 0)),
        out_shape=jax.ShapeDtypeStruct((t, d), F32),
        scratch_shapes=[pltpu.VMEM((tm, half), F32), pltpu.VMEM((tm, half), BF16)],
        compiler_params=_cparams(1),
        name="sgu",
    )(h, w_in, ln_g, ln_b, w_s, b_s_t, w_out, g, b)


def _kv_body(h_ref, w_ref, cmp_ref, kv_ref):
    xb = h_ref[0].astype(BF16)
    ts = xb.shape[0]
    n_cmp_cols = cmp_ref.shape[-1]
    cmp_ref[0] = _dot(xb, w_ref[:, :n_cmp_cols])
    gw = N_KV_GROUPS * HEAD_DIM
    pad = LANES - HEAD_DIM
    pos = pl.program_id(1) * ts + lax.broadcasted_iota(jnp.int32, (ts, pad), 0)
    lane = lax.broadcasted_iota(jnp.int32, (ts, pad), 1) + HEAD_DIM
    bias_lanes = jnp.logical_and(lane >= BIAS_F0, lane < BIAS_F0 + 2)
    blk_lane = jnp.logical_and(lane < BIAS_F0, pos // SEL_LEN == lane - SEL_F0)
    one = lambda m: jnp.where(m, 1.0, 0.0).astype(BF16)
    ext = [one(jnp.logical_or(bias_lanes, blk_lane)), one(lane == HEAD_DIM),
           one(bias_lanes), one(lane == HEAD_DIM)]
    for s in range(kv_ref.shape[0]):
        y = _dot(xb, w_ref[:, n_cmp_cols + s * gw: n_cmp_cols + (s + 1) * gw]).astype(BF16)
        for g in range(N_KV_GROUPS):
            kv_ref[s, 0, g, :, :HEAD_DIM] = y[:, g * HEAD_DIM:(g + 1) * HEAD_DIM]
            kv_ref[s, 0, g, :, HEAD_DIM:] = ext[s]


def _kv_proj(h3, kv_w):
    bsz, seq, d = h3.shape
    gw = N_KV_GROUPS * HEAD_DIM
    ts = min(PROJ_ROWS, seq)
    return pl.pallas_call(
        _kv_body,
        grid=(bsz, seq // ts),
        in_specs=[pl.BlockSpec((1, ts, d), lambda b, i: (b, i, 0)), _const_spec(kv_w.shape)],
        out_specs=[
            pl.BlockSpec((1, ts, 2 * gw), lambda b, i: (b, i, 0)),
            pl.BlockSpec((4, 1, N_KV_GROUPS, ts, LANES), lambda b, i: (0, b, 0, i, 0)),
        ],
        out_shape=[
            jax.ShapeDtypeStruct((bsz, seq, 2 * gw), F32),
            jax.ShapeDtypeStruct((4, bsz, N_KV_GROUPS, seq, LANES), BF16),
        ],
        compiler_params=_cparams(2),
        name="kv_proj",
    )(h3, kv_w)


def _cmp_body(r_ref, pe_ref, w1_ref, b1_ref, w2_ref, o_ref):
    n_chunks = r_ref.shape[3]
    rows = N_KV_GROUPS * n_chunks
    half = r_ref.shape[4]
    r = r_ref[0, 0].reshape(rows, half)
    top = (r + pe_ref[0, :, :half]).astype(BF16)
    bot = (r + pe_ref[0, :, half:]).astype(BF16)
    a = _dot(top, w1_ref[0, :half])
    bm = _dot(bot, w1_ref[0, half:])
    pre = a + pltpu.roll(bm, rows - 1, 0) + b1_ref[0]
    hid = _gelu(pre).astype(BF16)
    o_ref[0, 0] = _dot(hid, w2_ref[0]).astype(BF16).reshape(N_KV_GROUPS, n_chunks, LANES)


def _compress(r, pe, w1, b1, w2):
    _, bsz, g, n_chunks, width = r.shape
    phi = w1.shape[-1]
    return pl.pallas_call(
        _cmp_body,
        grid=(2, bsz),
        in_specs=[
            pl.BlockSpec((1, 1, g, n_chunks, width), lambda s, b: (s, b, 0, 0, 0)),
            pl.BlockSpec((1, 1, 2 * width), lambda s, b: (s, 0, 0)),
            pl.BlockSpec((1, 2 * width, phi), lambda s, b: (s, 0, 0)),
            pl.BlockSpec((1, 1, phi), lambda s, b: (s, 0, 0)),
            pl.BlockSpec((1, phi, LANES), lambda s, b: (s, 0, 0)),
        ],
        out_specs=pl.BlockSpec((1, 1, g, n_chunks, LANES), lambda s, b: (s, b, 0, 0, 0)),
        out_shape=jax.ShapeDtypeStruct((2, bsz, g, n_chunks, LANES), BF16),
        compiler_params=_cparams(2),
        name="compress",
    )(r, pe, w1, b1, w2)


def _qg_body(h_ref, wq_ref, wg_ref, cf_ref, q_ref, gate_ref):
    xb = h_ref[0].astype(BF16)
    ts = xb.shape[0]
    q = (_dot(xb, wq_ref[...]) * (HEAD_DIM ** -0.5 * LOG2E)).astype(BF16)
    for hd in range(N_HEADS):
        q_ref[0, hd, :, :HEAD_DIM] = q[:, hd * HEAD_DIM:(hd + 1) * HEAD_DIM]
        q_ref[0, hd, :, HEAD_DIM:] = jnp.broadcast_to(cf_ref[hd], (ts, LANES - HEAD_DIM))
    gate = jax.nn.sigmoid(_dot(xb, wg_ref[...]))
    gcols = gate_ref.shape[-1]
    for g in range(N_KV_GROUPS):
        gate_ref[0, g] = gate[:, g * gcols:(g + 1) * gcols]


def _qg_proj(h3, wq, wg, cfeat):
    bsz, seq, d = h3.shape
    ts = min(PROJ_ROWS, seq)
    gcols = wg.shape[1] // N_KV_GROUPS
    return pl.pallas_call(
        _qg_body,
        grid=(bsz, seq // ts),
        in_specs=[pl.BlockSpec((1, ts, d), lambda b, i: (b, i, 0)),
                  _const_spec(wq.shape), _const_spec(wg.shape), _const_spec(cfeat.shape)],
        out_specs=[
            pl.BlockSpec((1, N_HEADS, ts, LANES), lambda b, i: (b, 0, i, 0)),
            pl.BlockSpec((1, N_KV_GROUPS, ts, gcols), lambda b, i: (b, 0, i, 0)),
        ],
        out_shape=[
            jax.ShapeDtypeStruct((bsz, N_HEADS, seq, LANES), BF16),
            jax.ShapeDtypeStruct((bsz, N_KV_GROUPS, seq, gcols), F32),
        ],
        compiler_params=_cparams(2),
        name="qg_proj",
    )(h3, wq, wg, cfeat)


def _attn_body(q_ref, kc_ref, vc_ref, ks_ref, vs_ref, kw_ref, vw_ref, bc_ref, bd_ref,
               gate_ref, ov_ref, o_ref,
               ps_ref, pc_ref, oc_ref, qa_ref,
               ss_ref, sp_ref, sl_ref, sm_ref, sa_ref,
               ws_ref, wp_ref, wl_ref, wm_ref, wa_ref):
    hg, tq = q_ref.shape[1], q_ref.shape[2]
    rows = hg * tq
    n_cmp = kc_ref.shape[2]
    n_sel = ov_ref.shape[0]
    qt = pl.program_id(2)
    q0 = qt * tq
    rb_per_head = tq // ATT_ROWS

    ss_ref[:, :n_cmp] = _dot_nt(q_ref[0].reshape(rows, LANES), kc_ref[0, 0])
    for rb in range(rows // ATT_ROWS):
        h, part = divmod(rb, rb_per_head)
        rs = slice(rb * ATT_ROWS, (rb + 1) * ATT_ROWS)
        qs = slice(part * ATT_ROWS, (part + 1) * ATT_ROWS)
        s = ss_ref[rs, :n_cmp] + bc_ref[0, h, qs, :]
        m = jnp.maximum(jnp.max(s, axis=-1, keepdims=True), M_INIT)
        e = jnp.exp2(s - m)
        l = jnp.sum(e, axis=-1, keepdims=True)
        p = e / jnp.where(l > 0.0, l, 1.0)
        pc_ref[rs, :] = p.astype(BF16)
        if h == 0:
            ps_ref[qs, :] = p
        else:
            ps_ref[qs, :] += p
    oc_ref[...] = _dot(pc_ref[...], vc_ref[0, 0])

    psum = ps_ref[...]
    p_hi = psum.astype(BF16)
    r1 = psum - p_hi.astype(F32)
    p_mid = r1.astype(BF16)
    p_lo = (r1 - p_mid.astype(F32)).astype(BF16)
    ov = ov_ref[...]
    imp = _dot_nt(ov, p_hi) + _dot_nt(ov, p_mid) + _dot_nt(ov, p_lo)
    blk = lax.broadcasted_iota(jnp.int32, (n_sel, tq), 0)
    cur = (q0 + lax.broadcasted_iota(jnp.int32, (n_sel, tq), 1)) // SEL_LEN
    valid = blk <= cur
    forced = jnp.logical_and(valid, jnp.logical_or(blk == 0, blk > cur - N_LOCAL_BLOCKS))
    val = jnp.where(forced, -NEG, jnp.where(valid, imp, NEG))
    rank = jnp.zeros((n_sel, tq), F32)
    for j in range(n_sel):
        rj = val[j:j + 1, :]
        ahead = jnp.logical_or(rj > val, jnp.logical_and(rj == val, blk > j))
        rank = rank + jnp.where(ahead, 1.0, 0.0)
    feat = jnp.where(rank < float(min(SEL_TOP_N, n_sel)), 0.0, NEG)
    feat = jnp.concatenate([jnp.zeros((SEL_F0, tq), F32), feat,
                            jnp.zeros((LANES - SEL_F0 - n_sel, tq), F32)], axis=0)
    feat_t = feat.T
    for h in range(hg):
        qa_ref[h * tq:(h + 1) * tq, :] = (q_ref[0, h].astype(F32) + feat_t).astype(BF16)

    sel_state = (ss_ref, sp_ref, sl_ref, sm_ref, sa_ref)
    win_state = (ws_ref, wp_ref, wl_ref, wm_ref, wa_ref)
    for _, _, _, m_ref, a_ref in (sel_state, win_state):
        m_ref[...] = jnp.full(m_ref.shape, M_INIT, F32)
        a_ref[...] = jnp.zeros(a_ref.shape, F32)

    def tile(k_ref, v_ref, start, slot, state):
        s_ref, p_ref, l_ref, m_ref, a_ref = state
        start = pl.multiple_of(start, tq)
        s_ref[...] = _dot_nt(qa_ref[...], k_ref[0, 0, pl.ds(start, tq), :])
        for rb in range(rows // ATT_ROWS):
            h, part = divmod(rb, rb_per_head)
            rs = slice(rb * ATT_ROWS, (rb + 1) * ATT_ROWS)
            qs = slice(part * ATT_ROWS, (part + 1) * ATT_ROWS)
            pieces = []
            for c in range(tq // LANES):
                cs = slice(c * LANES, (c + 1) * LANES)
                sc = s_ref[rs, cs]
                if slot is not None:
                    sc = sc + bd_ref[h, slot, qs, cs]
                pieces.append(sc)
            mx = pieces[0]
            for sc in pieces[1:]:
                mx = jnp.maximum(mx, sc)
            m_old = m_ref[rs, :]
            m_new = jnp.maximum(m_old, jnp.max(mx, axis=-1, keepdims=True))
            l_ref[rs, :] = jnp.exp2(m_old - m_new)
            m_ref[rs, :] = m_new
            for c, sc in enumerate(pieces):
                p_ref[rs, c * LANES:(c + 1) * LANES] = jnp.exp2(sc - m_new).astype(BF16)
        pv = _dot(p_ref[...], v_ref[0, 0, pl.ds(start, tq), :])
        a_ref[...] = l_ref[...] * a_ref[...] + pv

    def far_step(p, carry):
        tile(ks_ref, vs_ref, p * tq, None, sel_state)
        return carry

    lax.fori_loop(0, jnp.maximum(qt - 1, 0), far_step, 0)

    @pl.when(qt >= 1)
    def _():
        tile(ks_ref, vs_ref, (qt - 1) * tq, 1, sel_state)
        tile(kw_ref, vw_ref, (qt - 1) * tq, 1, win_state)

    @pl.when(qt >= 2)
    def _():
        tile(kw_ref, vw_ref, (qt - 2) * tq, 2, win_state)

    tile(ks_ref, vs_ref, q0, 0, sel_state)
    tile(kw_ref, vw_ref, q0, 0, win_state)

    gates = gate_ref[0, 0]
    for h in range(hg):
        hs = slice(h * tq, (h + 1) * tq)
        a_s = sa_ref[hs, :]
        a_w = wa_ref[hs, :]
        l_s = a_s[:, HEAD_DIM:HEAD_DIM + 1]
        l_w = a_w[:, HEAD_DIM:HEAD_DIM + 1]
        o_s = a_s[:, :HEAD_DIM] / jnp.where(l_s > 0.0, l_s, 1.0)
        o_w = a_w[:, :HEAD_DIM] / jnp.where(l_w > 0.0, l_w, 1.0)
        out = (gates[:, 3 * h:3 * h + 1] * oc_ref[hs, :HEAD_DIM]
               + gates[:, 3 * h + 1:3 * h + 2] * o_s + gates[:, 3 * h + 2:3 * h + 3] * o_w)
        o_ref[0, :, h * HEAD_DIM:(h + 1) * HEAD_DIM] = out.astype(BF16)


def _attention(q, kc, vc, kv4, bias_c, bias_d, gates, ov_t):
    bsz, _, seq, _ = q.shape
    tq = min(ATT_Q, seq)
    hg = HEADS_PER_GROUP
    rows = hg * tq
    n_cmp = kc.shape[2]
    assert ov_t.shape[0] <= MAX_SEL_BLOCKS and n_cmp <= tq and tq % LANES == 0
    kv_spec = lambda slot: pl.BlockSpec(
        (1, 1, seq, LANES), lambda b, g, i, slot=slot: (slot * bsz + b, g, 0, 0))
    kv_flat = kv4.reshape(4 * bsz, N_KV_GROUPS, seq, LANES)
    cmp_spec = pl.BlockSpec((1, 1, n_cmp, LANES), lambda b, g, i: (b, g, 0, 0))
    branch_scratch = [pltpu.VMEM((rows, tq), F32), pltpu.VMEM((rows, tq), BF16),
                      pltpu.VMEM((rows, LANES), F32), pltpu.VMEM((rows, LANES), F32),
                      pltpu.VMEM((rows, LANES), F32)]
    return pl.pallas_call(
        _attn_body,
        grid=(bsz, N_KV_GROUPS, seq // tq),
        in_specs=[
            pl.BlockSpec((1, hg, tq, LANES), lambda b, g, i: (b, g, i, 0)),
            cmp_spec, cmp_spec,
            kv_spec(0), kv_spec(1), kv_spec(2), kv_spec(3),
            pl.BlockSpec((1, hg, tq, n_cmp), lambda b, g, i: (i, g, 0, 0)),
            pl.BlockSpec((hg, 3, tq, tq), lambda b, g, i: (g, 0, 0, 0)),
            pl.BlockSpec((1, 1, tq, gates.shape[-1]), lambda b, g, i: (b, g, i, 0)),
            _const_spec(ov_t.shape),
        ],
        out_specs=pl.BlockSpec((1, tq, hg * HEAD_DIM), lambda b, g, i: (b, i, g)),
        out_shape=jax.ShapeDtypeStruct((bsz, seq, N_HEADS * HEAD_DIM), BF16),
        scratch_shapes=[
            pltpu.VMEM((tq, n_cmp), F32), pltpu.VMEM((rows, n_cmp), BF16),
            pltpu.VMEM((rows, LANES), F32), pltpu.VMEM((rows, LANES), BF16),
        ] + branch_scratch + branch_scratch,
        compiler_params=_cparams(3),
        name="nsa_attn",
    )(q, kc, vc, kv_flat, kv_flat, kv_flat, kv_flat, bias_c, bias_d, gates, ov_t)


def _oproj_body(h_ref, a_ref, w_ref, g_ref, b_ref, o_ref):
    mix = _dot(a_ref[...], w_ref[...])
    o_ref[...] = _layer_norm(DN_ALPHA * h_ref[...] + mix, g_ref[...], b_ref[...])


def _oproj(h, a, w_o, g, b):
    t, d = h.shape
    tm = min(PROJ_ROWS, t)
    return pl.pallas_call(
        _oproj_body,
        grid=(t // tm,),
        in_specs=[
            pl.BlockSpec((tm, d), lambda i: (i, 0)),
            pl.BlockSpec((tm, a.shape[1]), lambda i: (i, 0)),
            _const_spec(w_o.shape), _const_spec((1, d)), _const_spec((1, d)),
        ],
        out_specs=pl.BlockSpec((tm, d), lambda i: (i, 0)),
        out_shape=jax.ShapeDtypeStruct((t, d), F32),
        compiler_params=_cparams(1),
        name="oproj",
    )(h, a, w_o, g, b)


def _rel_bucket_np(dist):
    n = np.maximum(dist, 0)
    max_exact = REL_BUCKETS // 2
    nf = np.maximum(n, 1).astype(np.float32)
    large = max_exact + (np.log(nf / np.float32(max_exact)) / np.float32(math.log(REL_MAX_DIST / max_exact))
                         * np.float32(REL_BUCKETS - max_exact)).astype(np.int32)
    return np.where(n < max_exact, n, np.minimum(large, REL_BUCKETS - 1)).astype(np.int32)


def _bias_tables(rel_table, seq, tq, n_cmp_pad):
    tbl2 = rel_table * LOG2E
    far = REL_BUCKETS - 1
    c_hi = tbl2[far].astype(BF16)
    c_lo = (tbl2[far] - c_hi.astype(F32)).astype(BF16)
    c2 = c_hi.astype(F32) + c_lo.astype(F32)
    q = np.arange(tq)[:, None]
    k = np.arange(tq)[None, :]
    d0 = q - k
    d1 = q - k + tq
    d2 = q - k + 2 * tq
    assert tq >= REL_MAX_DIST and 2 * tq == WINDOW
    buckets = np.stack([_rel_bucket_np(d0), _rel_bucket_np(d1), _rel_bucket_np(d2)])
    visible = np.stack([d0 >= 0, np.ones((tq, tq), bool), d2 < WINDOW])
    bias_d = jnp.where(visible[..., None], tbl2[buckets] - c2, NEG)
    bias_d = bias_d.transpose(3, 0, 1, 2)

    n_qt = seq // tq
    t = (np.arange(n_qt)[:, None, None] * tq + np.arange(tq)[None, :, None])
    i = np.arange(n_cmp_pad)[None, None, :]
    dc = t - (i * CMP_STRIDE + CMP_LEN - 1)
    n_cmp = seq // CMP_STRIDE - CMP_LEN // CMP_STRIDE + 1
    vis_c = np.logical_and(dc >= 0, i < n_cmp)
    bias_c = jnp.where(vis_c[..., None], tbl2[_rel_bucket_np(dc)], NEG)
    bias_c = bias_c.transpose(0, 3, 1, 2)

    cfeat = jnp.zeros((N_HEADS, 1, LANES - HEAD_DIM), BF16)
    cfeat = cfeat.at[:, 0, BIAS_F0 - HEAD_DIM].set(c_hi).at[:, 0, BIAS_F0 - HEAD_DIM + 1].set(c_lo)
    return bias_c, bias_d, cfeat


def _overlap_t(n_cmp_pad, n_sel):
    n_cmp = n_cmp_pad - CMP_LEN // CMP_STRIDE + 1
    i = np.arange(n_cmp_pad)[None, :]
    j = np.arange(n_sel)[:, None]
    ov = (i * CMP_STRIDE < (j + 1) * SEL_LEN) & (i * CMP_STRIDE + CMP_LEN > j * SEL_LEN) & (i < n_cmp)
    return ov.astype(np.float32)


def kernel(x, rel_table, ln_g, ln_b, ffn_w1, ffn_w3, ffn_w2, sgu_w_in, sgu_ln_g, sgu_ln_b, sgu_w_s,
           sgu_b_s, sgu_w_out, kv_w, cmp_pe, cmp_w1, cmp_b1, cmp_w2, nsa_w_qg, nsa_w_o):
    bsz, seq, d = x.shape
    t = bsz * seq
    h = x.reshape(t, d)
    n_chunks = seq // CMP_STRIDE
    tq = min(ATT_Q, seq)

    def ffn(h, layer, half, norm):
        return _ffn(h, ffn_w1[layer, half].astype(BF16), ffn_w3[layer, half].astype(BF16),
                    ffn_w2[layer, half].astype(BF16), ln_g[layer, norm][None], ln_b[layer, norm][None])

    shared = None
    for layer in range(DEPTH):
        if layer == N_A_LAYERS:
            kv_cmp, kv4 = _kv_proj(h.reshape(bsz, seq, d), kv_w.astype(BF16))
            r = kv_cmp.reshape(bsz, n_chunks, CMP_STRIDE, 2, N_KV_GROUPS, HEAD_DIM)
            r = r.transpose(3, 0, 4, 1, 2, 5).reshape(2, bsz, N_KV_GROUPS, n_chunks, CMP_STRIDE * HEAD_DIM)
            w2_pad = jnp.pad(cmp_w2, ((0, 0), (0, 0), (0, LANES - HEAD_DIM))).astype(BF16)
            kvc = _compress(r, cmp_pe.reshape(2, 1, CMP_LEN * HEAD_DIM), cmp_w1.astype(BF16),
                            cmp_b1[:, None, :], w2_pad)
            bias_c, bias_d, cfeat = _bias_tables(rel_table, seq, tq, n_chunks)
            ov_t = jnp.asarray(_overlap_t(n_chunks, seq // SEL_LEN), BF16)
            shared = (kvc[0], kvc[1], kv4, bias_c, bias_d, cfeat, ov_t)
        h = ffn(h, layer, 0, 0)
        if layer < N_A_LAYERS:
            a = layer
            h = _sgu(h, sgu_w_in[a].astype(BF16), sgu_ln_g[a][None], sgu_ln_b[a][None], sgu_w_s[a],
                     sgu_b_s[a].T, sgu_w_out[a].astype(BF16), ln_g[layer, 1][None], ln_b[layer, 1][None])
        else:
            bl = layer - N_A_LAYERS
            kc, vc, kv4, bias_c, bias_d, cfeat, ov_t = shared
            nq = N_HEADS * HEAD_DIM
            wq = nsa_w_qg[bl][:, :nq].astype(BF16)
            wg = nsa_w_qg[bl][:, nq:].reshape(d, N_KV_GROUPS, 3 * HEADS_PER_GROUP)
            wg = jnp.pad(wg, ((0, 0), (0, 0), (0, GATE_COLS - 3 * HEADS_PER_GROUP)))
            wg = wg.reshape(d, N_KV_GROUPS * GATE_COLS).astype(BF16)
            q, gates = _qg_proj(h.reshape(bsz, seq, d), wq, wg, cfeat)
            att = _attention(q, kc, vc, kv4, bias_c, bias_d, gates, ov_t)
            h = _oproj(h, att.reshape(t, nq), nsa_w_o[bl].astype(BF16), ln_g[layer, 1][None], ln_b[layer, 1][None])
        h = ffn(h, layer, 1, 2)
    return h.reshape(bsz, seq, d)
```

```python
import math

import numpy as np
import jax
import jax.numpy as jnp
from jax import lax
from jax.experimental import pallas as pl
from jax.experimental.pallas import tpu as pltpu

DEPTH = 4
N_A_LAYERS = DEPTH // 2
DN_ALPHA = (2.0 * DEPTH) ** 0.25
LN_EPS = 1e-5
SGU_GROUPS = 8
SGU_CHUNK = 128
N_HEADS = 16
N_KV_GROUPS = 4
HEADS_PER_GROUP = N_HEADS // N_KV_GROUPS
HEAD_DIM = 64
CMP_LEN = 32
CMP_STRIDE = 16
SEL_LEN = 64
SEL_TOP_N = 8
N_LOCAL_BLOCKS = 2
WINDOW = 512
REL_BUCKETS = 32
REL_MAX_DIST = 128
NEG = -1e30
M_INIT = -1e20
LOG2E = 1.0 / math.log(2.0)

BF16 = jnp.bfloat16
F32 = jnp.float32

FFN_ROWS = 512
FFN_COLS = 256
SGU_ROWS = 512
PROJ_ROWS = 512
ATT_Q = 256
ATT_ROWS = 128
LANES = 128
SEL_F0 = HEAD_DIM
BIAS_F0 = 96
MAX_SEL_BLOCKS = BIAS_F0 - SEL_F0
GATE_COLS = 32
HEAD_ORDER = (0, 2, 1, 3)
N_BRANCH = 3
KV_SLOTS = 6
VMEM_LIMIT = 56 * 1024 * 1024


def _cparams(n_axes):
    return pltpu.CompilerParams(
        dimension_semantics=("arbitrary",) * n_axes, vmem_limit_bytes=VMEM_LIMIT)


def _const_spec(shape):
    nd = len(shape)
    return pl.BlockSpec(shape, lambda *_: (0,) * nd)


def _layer_norm(y, g, b):
    mu = jnp.mean(y, axis=-1, keepdims=True)
    yc = y - mu
    var = jnp.mean(yc * yc, axis=-1, keepdims=True)
    return yc * lax.rsqrt(var + LN_EPS) * g + b


def _gelu(x):
    return 0.5 * x * (1.0 + lax.erf(x * math.sqrt(0.5)))


def _dot(a, b):
    return jnp.dot(a, b, preferred_element_type=F32)


def _dot_nt(a, b):
    return lax.dot_general(a, b, (((1,), (1,)), ((), ())), preferred_element_type=F32)


def _ffn_body(h_ref, w1_ref, w3_ref, w2_ref, g_ref, b_ref, o_ref, hid_ref):
    x = h_ref[...]
    xb = x.astype(BF16)
    ffn = w1_ref.shape[1]
    for c in range(ffn // FFN_COLS):
        cs = slice(c * FFN_COLS, (c + 1) * FFN_COLS)
        a = _dot(xb, w1_ref[:, cs])
        b3 = _dot(xb, w3_ref[:, cs])
        hid_ref[:, cs] = (jax.nn.silu(a) * b3).astype(BF16)
    f = _dot(hid_ref[...], w2_ref[...])
    o_ref[...] = _layer_norm(DN_ALPHA * x + 0.5 * f, g_ref[...], b_ref[...])


def _ffn(h, w1, w3, w2, g, b):
    t, d = h.shape
    ffn = w1.shape[1]
    tm = min(FFN_ROWS, t)
    return pl.pallas_call(
        _ffn_body,
        grid=(t // tm,),
        in_specs=[
            pl.BlockSpec((tm, d), lambda i: (i, 0)),
            _const_spec((d, ffn)), _const_spec((d, ffn)), _const_spec((ffn, d)),
            _const_spec((1, d)), _const_spec((1, d)),
        ],
        out_specs=pl.BlockSpec((tm, d), lambda i: (i, 0)),
        out_shape=jax.ShapeDtypeStruct((t, d), F32),
        scratch_shapes=[pltpu.VMEM((tm, ffn), BF16)],
        compiler_params=_cparams(1),
        name="ffn",
    )(h, w1, w3, w2, g, b)


def _sgu_body(h_ref, win_ref, lg_ref, lb_ref, ws_ref, bs_ref, wout_ref, g_ref, b_ref,
              o_ref, v_ref, gat_ref):
    x = h_ref[...]
    xb = x.astype(BF16)
    tm = x.shape[0]
    half = wout_ref.shape[0]
    gd = half // SGU_GROUPS
    for c in range(SGU_GROUPS):
        cs = slice(c * gd, (c + 1) * gd)
        v_ref[:, cs] = _gelu(_dot(xb, win_ref[:, half + c * gd: half + (c + 1) * gd]))
    v = v_ref[...]
    mu = jnp.mean(v, axis=-1, keepdims=True)
    var = jnp.mean((v - mu) * (v - mu), axis=-1, keepdims=True)
    rstd = lax.rsqrt(var + LN_EPS)
    row = lax.broadcasted_iota(jnp.int32, (SGU_CHUNK, SGU_CHUNK), 0)
    col = lax.broadcasted_iota(jnp.int32, (SGU_CHUNK, SGU_CHUNK), 1)
    causal = col <= row
    for c in range(SGU_GROUPS):
        cs = slice(c * gd, (c + 1) * gd)
        vn = ((v_ref[:, cs] - mu) * rstd * lg_ref[:, cs] + lb_ref[:, cs]).astype(BF16)
        u = _gelu(_dot(xb, win_ref[:, cs]))
        w = jnp.where(causal, ws_ref[c], 0.0).astype(BF16)
        bias = bs_ref[:, c:c + 1]
        for r in range(tm // SGU_CHUNK):
            rs = slice(r * SGU_CHUNK, (r + 1) * SGU_CHUNK)
            mixed = _dot(w, vn[rs]) + bias
            gat_ref[rs, cs] = (u[rs] * mixed).astype(BF16)
    mix = _dot(gat_ref[...], wout_ref[...])
    o_ref[...] = _layer_norm(DN_ALPHA * x + mix, g_ref[...], b_ref[...])


def _sgu(h, w_in, ln_g, ln_b, w_s, b_s_t, w_out, g, b):
    t, d = h.shape
    hidden = w_in.shape[1]
    half = hidden // 2
    tm = min(SGU_ROWS, t)
    return pl.pallas_call(
        _sgu_body,
        grid=(t // tm,),
        in_specs=[
            pl.BlockSpec((tm, d), lambda i: (i, 0)),
            _const_spec((d, hidden)), _const_spec((1, half)), _const_spec((1, half)),
            _const_spec(w_s.shape), _const_spec(b_s_t.shape), _const_spec((half, d)),
            _const_spec((1, d)), _const_spec((1, d)),
        ],
        out_specs=pl.BlockSpec((tm, d), lambda i: (i, 0)),
        out_shape=jax.ShapeDtypeStruct((t, d), F32),
        scratch_shapes=[pltpu.VMEM((tm, half), F32), pltpu.VMEM((tm, half), BF16)],
        compiler_params=_cparams(1),
        name="sgu",
    )(h, w_in, ln_g, ln_b, w_s, b_s_t, w_out, g, b)


def _kv_body(h_ref, w_ref, cmp_ref, kv_ref):
    xb = h_ref[0].astype(BF16)
    ts = xb.shape[0]
    n_cmp_cols = cmp_ref.shape[-1]
    cmp_ref[0] = _dot(xb, w_ref[:, :n_cmp_cols])
    gw = N_KV_GROUPS * HEAD_DIM
    pad = LANES - HEAD_DIM
    pos = pl.program_id(1) * ts + lax.broadcasted_iota(jnp.int32, (ts, pad), 0)
    lane = lax.broadcasted_iota(jnp.int32, (ts, pad), 1) + HEAD_DIM
    bias_lanes = jnp.logical_and(lane >= BIAS_F0, lane < BIAS_F0 + 2)
    blk_lane = jnp.logical_and(lane < BIAS_F0, pos // SEL_LEN == lane - SEL_F0)
    one = lambda m: jnp.where(m, 1.0, 0.0).astype(BF16)
    k_ext = (one(jnp.logical_or(bias_lanes, blk_lane)), one(bias_lanes))
    ones = jnp.ones((ts, pad), BF16)
    for s in range(4):
        y = _dot(xb, w_ref[:, n_cmp_cols + s * gw: n_cmp_cols + (s + 1) * gw]).astype(BF16)
        out = 3 * (s // 2)
        for g in range(N_KV_GROUPS):
            piece = y[:, g * HEAD_DIM:(g + 1) * HEAD_DIM]
            if s % 2 == 0:
                kv_ref[out, 0, g, :, :HEAD_DIM] = piece
                kv_ref[out, 0, g, :, HEAD_DIM:] = k_ext[s // 2]
            else:
                kv_ref[out + 1, 0, g, :, :HEAD_DIM] = piece
                kv_ref[out + 1, 0, g, :, HEAD_DIM:] = ones
                kv_ref[out + 2, 0, g, :, :HEAD_DIM] = ones
                kv_ref[out + 2, 0, g, :, HEAD_DIM:] = piece


def _kv_proj(h3, kv_w):
    bsz, seq, d = h3.shape
    gw = N_KV_GROUPS * HEAD_DIM
    ts = min(PROJ_ROWS, seq)
    return pl.pallas_call(
        _kv_body,
        grid=(bsz, seq // ts),
        in_specs=[pl.BlockSpec((1, ts, d), lambda b, i: (b, i, 0)), _const_spec(kv_w.shape)],
        out_specs=[
            pl.BlockSpec((1, ts, 2 * gw), lambda b, i: (b, i, 0)),
            pl.BlockSpec((KV_SLOTS, 1, N_KV_GROUPS, ts, LANES), lambda b, i: (0, b, 0, i, 0)),
        ],
        out_shape=[
            jax.ShapeDtypeStruct((bsz, seq, 2 * gw), F32),
            jax.ShapeDtypeStruct((KV_SLOTS, bsz, N_KV_GROUPS, seq, LANES), BF16),
        ],
        compiler_params=_cparams(2),
        name="kv_proj",
    )(h3, kv_w)


def _cmp_body(r_ref, pe_ref, w1_ref, b1_ref, w2_ref, o_ref):
    n_chunks = r_ref.shape[3]
    rows = N_KV_GROUPS * n_chunks
    half = r_ref.shape[4]
    r = r_ref[0, 0].reshape(rows, half)
    top = (r + pe_ref[0, :, :half]).astype(BF16)
    bot = (r + pe_ref[0, :, half:]).astype(BF16)
    a = _dot(top, w1_ref[0, :half])
    bm = _dot(bot, w1_ref[0, half:])
    pre = a + pltpu.roll(bm, rows - 1, 0) + b1_ref[0]
    hid = _gelu(pre).astype(BF16)
    o_ref[0, 0] = _dot(hid, w2_ref[0]).astype(BF16).reshape(N_KV_GROUPS, n_chunks, LANES)


def _compress(r, pe, w1, b1, w2):
    _, bsz, g, n_chunks, width = r.shape
    phi = w1.shape[-1]
    n_out = w2.shape[0]
    src = lambda s: jnp.minimum(s, 1)
    return pl.pallas_call(
        _cmp_body,
        grid=(n_out, bsz),
        in_specs=[
            pl.BlockSpec((1, 1, g, n_chunks, width), lambda s, b: (src(s), b, 0, 0, 0)),
            pl.BlockSpec((1, 1, 2 * width), lambda s, b: (src(s), 0, 0)),
            pl.BlockSpec((1, 2 * width, phi), lambda s, b: (src(s), 0, 0)),
            pl.BlockSpec((1, 1, phi), lambda s, b: (src(s), 0, 0)),
            pl.BlockSpec((1, phi, LANES), lambda s, b: (s, 0, 0)),
        ],
        out_specs=pl.BlockSpec((1, 1, g, n_chunks, LANES), lambda s, b: (s, b, 0, 0, 0)),
        out_shape=jax.ShapeDtypeStruct((n_out, bsz, g, n_chunks, LANES), BF16),
        compiler_params=_cparams(2),
        name="compress",
    )(r, pe, w1, b1, w2)


def _qg_body(h_ref, wq_ref, wg_ref, cf_ref, q_ref, gate_ref):
    xb = h_ref[0].astype(BF16)
    ts = xb.shape[0]
    q = (_dot(xb, wq_ref[...]) * (HEAD_DIM ** -0.5 * LOG2E)).astype(BF16)
    for slot in range(N_HEADS):
        hd = slot - slot % HEADS_PER_GROUP + HEAD_ORDER[slot % HEADS_PER_GROUP]
        q_ref[0, slot, :, :HEAD_DIM] = q[:, hd * HEAD_DIM:(hd + 1) * HEAD_DIM]
        q_ref[0, slot, :, HEAD_DIM:] = jnp.broadcast_to(cf_ref[hd], (ts, LANES - HEAD_DIM))
    gate = jax.nn.sigmoid(_dot(xb, wg_ref[...]))
    gcols = gate_ref.shape[-1]
    for g in range(N_KV_GROUPS):
        gate_ref[0, g] = gate[:, g * gcols:(g + 1) * gcols]


def _qg_proj(h3, wq, wg, cfeat):
    bsz, seq, d = h3.shape
    ts = min(PROJ_ROWS, seq)
    gcols = wg.shape[1] // N_KV_GROUPS
    return pl.pallas_call(
        _qg_body,
        grid=(bsz, seq // ts),
        in_specs=[pl.BlockSpec((1, ts, d), lambda b, i: (b, i, 0)),
                  _const_spec(wq.shape), _const_spec(wg.shape), _const_spec(cfeat.shape)],
        out_specs=[
            pl.BlockSpec((1, N_HEADS, ts, LANES), lambda b, i: (b, 0, i, 0)),
            pl.BlockSpec((1, N_KV_GROUPS, ts, gcols), lambda b, i: (b, 0, i, 0)),
        ],
        out_shape=[
            jax.ShapeDtypeStruct((bsz, N_HEADS, seq, LANES), BF16),
            jax.ShapeDtypeStruct((bsz, N_KV_GROUPS, seq, gcols), F32),
        ],
        compiler_params=_cparams(2),
        name="qg_proj",
    )(h3, wq, wg, cfeat)


def _attn_body(q_ref, kc_ref, vce_ref, vco_ref, ks_ref, vse_ref, vso_ref, kw_ref, vwe_ref, vwo_ref,
               bc_ref, bd_ref, gate_ref, ov_ref, gx_ref, o_ref,
               ps_ref, pc_ref, oc_ref, qa_ref,
               ss_ref, sp_ref, sl_ref, sm_ref, sa_ref,
               ws_ref, wp_ref, wl_ref, wm_ref, wa_ref):
    hg, tq = q_ref.shape[1], q_ref.shape[2]
    rows = hg * tq
    half_rows = rows // 2
    n_cmp = kc_ref.shape[2]
    n_sel = ov_ref.shape[0]
    qt = pl.program_id(2)
    q0 = qt * tq
    rb_per_head = tq // ATT_ROWS
    halves = (slice(0, half_rows), slice(half_rows, rows))

    def q_rows(src_ref, hv):
        if src_ref is q_ref:
            return q_ref[0, hv * (hg // 2):(hv + 1) * (hg // 2)].reshape(half_rows, LANES)
        return src_ref[halves[hv], :]

    def row_blocks(hv):
        out = []
        for rb in range(hv * half_rows // ATT_ROWS, (hv + 1) * half_rows // ATT_ROWS):
            slot, part = divmod(rb, rb_per_head)
            out.append((slice(rb * ATT_ROWS, (rb + 1) * ATT_ROWS), HEAD_ORDER[slot],
                        slice(part * ATT_ROWS, (part + 1) * ATT_ROWS)))
        return out

    sel_state = (ss_ref, sp_ref, sl_ref, sm_ref, sa_ref)
    win_state = (ws_ref, wp_ref, wl_ref, wm_ref, wa_ref)
    for _, _, _, m_ref, a_ref in (sel_state, win_state):
        m_ref[...] = jnp.full(m_ref.shape, M_INIT, F32)
        a_ref[...] = jnp.zeros(a_ref.shape, F32)

    def tile(src_ref, k_ref, v_refs, start, slot, state):
        s_ref, p_ref, l_ref, m_ref, a_ref = state
        start = pl.multiple_of(start, tq)
        k = k_ref[0, 0, pl.ds(start, tq), :]
        for hv in range(2):
            hs = halves[hv]
            s_ref[hs, :] = _dot_nt(q_rows(src_ref, hv), k)
            for rs, head, qs in row_blocks(hv):
                pieces = []
                for c in range(tq // LANES):
                    cs = slice(c * LANES, (c + 1) * LANES)
                    sc = s_ref[rs, cs]
                    if slot is not None:
                        sc = sc + bd_ref[head, slot, qs, cs]
                    pieces.append(sc)
                mx = pieces[0]
                for sc in pieces[1:]:
                    mx = jnp.maximum(mx, sc)
                m_old = m_ref[rs, :]
                m_new = jnp.maximum(m_old, jnp.max(mx, axis=-1, keepdims=True))
                l_ref[rs, :] = jnp.exp2(m_old - m_new)
                m_ref[rs, :] = m_new
                for c, sc in enumerate(pieces):
                    p_ref[rs, c * LANES:(c + 1) * LANES] = jnp.exp2(sc - m_new).astype(BF16)
            pv = _dot(p_ref[hs, :], v_refs[hv][0, 0, pl.ds(start, tq), :])
            a_ref[hs, :] = l_ref[hs, :] * a_ref[hs, :] + pv

    sel_kv = (ks_ref, (vse_ref, vso_ref))
    win_kv = (kw_ref, (vwe_ref, vwo_ref))

    kc = kc_ref[0, 0]
    for hv, vc_ref in enumerate((vce_ref, vco_ref)):
        ss_ref[halves[hv], :n_cmp] = _dot_nt(q_rows(q_ref, hv), kc)
        for rs, head, qs in row_blocks(hv):
            s = ss_ref[rs, :n_cmp] + bc_ref[0, head, qs, :]
            m = jnp.maximum(jnp.max(s, axis=-1, keepdims=True), M_INIT)
            e = jnp.exp2(s - m)
            l = jnp.sum(e, axis=-1, keepdims=True)
            p = e / jnp.where(l > 0.0, l, 1.0)
            pc_ref[rs, :] = p.astype(BF16)
            if head == 0:
                ps_ref[qs, :] = p
            else:
                ps_ref[qs, :] += p
        oc_ref[halves[hv], :] = _dot(pc_ref[halves[hv], :], vc_ref[0, 0])
    tile(q_ref, *win_kv, q0, 0, win_state)

    psum = ps_ref[...]
    p_hi = psum.astype(BF16)
    r1 = psum - p_hi.astype(F32)
    p_mid = r1.astype(BF16)
    p_lo = (r1 - p_mid.astype(F32)).astype(BF16)
    ov = ov_ref[...]
    imp = _dot_nt(ov, p_hi) + _dot_nt(ov, p_mid) + _dot_nt(ov, p_lo)
    blk = lax.broadcasted_iota(jnp.int32, (n_sel, tq), 0)
    cur = (q0 + lax.broadcasted_iota(jnp.int32, (n_sel, tq), 1)) // SEL_LEN
    valid = blk <= cur
    forced = jnp.logical_and(valid, jnp.logical_or(blk == 0, blk > cur - N_LOCAL_BLOCKS))
    val = jnp.where(forced, -NEG, jnp.where(valid, imp, NEG))
    rank = jnp.zeros((n_sel, tq), F32)
    for j in range(n_sel):
        rj = val[j:j + 1, :]
        ahead = jnp.logical_or(rj > val, jnp.logical_and(rj == val, blk > j))
        rank = rank + jnp.where(ahead, 1.0, 0.0)
    feat = jnp.where(rank < float(min(SEL_TOP_N, n_sel)), 0.0, NEG)
    feat = jnp.concatenate([jnp.zeros((SEL_F0, tq), F32), feat,
                            jnp.zeros((LANES - SEL_F0 - n_sel, tq), F32)], axis=0)
    feat_t = feat.T
    for slot in range(hg):
        qa_ref[slot * tq:(slot + 1) * tq, :] = (q_ref[0, slot].astype(F32) + feat_t).astype(BF16)

    @pl.when(qt >= 2)
    def _():
        tile(q_ref, *win_kv, (qt - 2) * tq, 2, win_state)
        tile(qa_ref, *sel_kv, 0, None, sel_state)

    def far_step(p, carry):
        tile(qa_ref, *sel_kv, p * tq, None, sel_state)
        return carry

    lax.fori_loop(1, jnp.maximum(qt - 1, 1), far_step, 0)

    @pl.when(qt >= 1)
    def _():
        tile(qa_ref, *sel_kv, (qt - 1) * tq, 1, sel_state)
        tile(q_ref, *win_kv, (qt - 1) * tq, 1, win_state)

    tile(qa_ref, *sel_kv, q0, 0, sel_state)

    gates = gate_ref[0, 0]
    g_hi = gates.astype(BF16)
    g_lo = (gates - g_hi.astype(F32)).astype(BF16)
    gx = _dot(g_hi, gx_ref[...]) + _dot(g_lo, gx_ref[...])
    low = lax.broadcasted_iota(jnp.int32, (tq, LANES), 1) < HEAD_DIM
    n_pairs = hg // 2
    for j in range(n_pairs):
        even = slice(j * tq, (j + 1) * tq)
        odd = slice(half_rows + j * tq, half_rows + (j + 1) * tq)
        out = gx[:, j * LANES:(j + 1) * LANES] * (oc_ref[even, :] + oc_ref[odd, :])
        for br, a_ref in ((1, sa_ref), (2, wa_ref)):
            a_e = a_ref[even, :]
            a_o = a_ref[odd, :]
            num = jnp.where(low, a_e, a_o)
            den = pltpu.roll(jnp.where(low, a_o, a_e), HEAD_DIM, 1)
            col = (br * n_pairs + j) * LANES
            out = out + gx[:, col:col + LANES] * (num / jnp.where(den > 0.0, den, 1.0))
        o_ref[0, :, j * LANES:(j + 1) * LANES] = out.astype(BF16)


def _attention(q, kvc, kv6, bias_c, bias_d, gates, ov_t, gate_expand):
    bsz, _, seq, _ = q.shape
    tq = min(ATT_Q, seq)
    hg = HEADS_PER_GROUP
    rows = hg * tq
    n_cmp = kvc.shape[3]
    assert ov_t.shape[0] <= MAX_SEL_BLOCKS and n_cmp <= tq and tq % LANES == 0
    kv_spec = lambda slot: pl.BlockSpec(
        (1, 1, seq, LANES), lambda b, g, i, slot=slot: (slot * bsz + b, g, 0, 0))
    kv_flat = kv6.reshape(KV_SLOTS * bsz, N_KV_GROUPS, seq, LANES)
    cmp_spec = lambda slot: pl.BlockSpec(
        (1, 1, n_cmp, LANES), lambda b, g, i, slot=slot: (slot * bsz + b, g, 0, 0))
    cmp_flat = kvc.reshape(3 * bsz, N_KV_GROUPS, n_cmp, LANES)
    branch_scratch = [pltpu.VMEM((rows, tq), F32), pltpu.VMEM((rows, tq), BF16),
                      pltpu.VMEM((rows, LANES), F32), pltpu.VMEM((rows, LANES), F32),
                      pltpu.VMEM((rows, LANES), F32)]
    return pl.pallas_call(
        _attn_body,
        grid=(bsz, N_KV_GROUPS, seq // tq),
        in_specs=[
            pl.BlockSpec((1, hg, tq, LANES), lambda b, g, i: (b, g, i, 0)),
            cmp_spec(0), cmp_spec(1), cmp_spec(2),
            kv_spec(0), kv_spec(1), kv_spec(2), kv_spec(3), kv_spec(4), kv_spec(5),
            pl.BlockSpec((1, hg, tq, n_cmp), lambda b, g, i: (i, g, 0, 0)),
            pl.BlockSpec((hg, 3, tq, tq), lambda b, g, i: (g, 0, 0, 0)),
            pl.BlockSpec((1, 1, tq, gates.shape[-1]), lambda b, g, i: (b, g, i, 0)),
            _const_spec(ov_t.shape), _const_spec(gate_expand.shape),
        ],
        out_specs=pl.BlockSpec((1, tq, hg * HEAD_DIM), lambda b, g, i: (b, i, g)),
        out_shape=jax.ShapeDtypeStruct((bsz, seq, N_HEADS * HEAD_DIM), BF16),
        scratch_shapes=[
            pltpu.VMEM((tq, n_cmp), F32), pltpu.VMEM((rows, n_cmp), BF16),
            pltpu.VMEM((rows, LANES), F32), pltpu.VMEM((rows, LANES), BF16),
        ] + branch_scratch + branch_scratch,
        compiler_params=_cparams(3),
        name="nsa_attn",
    )(q, cmp_flat, cmp_flat, cmp_flat, *([kv_flat] * KV_SLOTS), bias_c, bias_d, gates, ov_t, gate_expand)


def _oproj_body(h_ref, a_ref, w_ref, g_ref, b_ref, o_ref):
    mix = _dot(a_ref[...], w_ref[...])
    o_ref[...] = _layer_norm(DN_ALPHA * h_ref[...] + mix, g_ref[...], b_ref[...])


def _oproj(h, a, w_o, g, b):
    t, d = h.shape
    tm = min(PROJ_ROWS, t)
    return pl.pallas_call(
        _oproj_body,
        grid=(t // tm,),
        in_specs=[
            pl.BlockSpec((tm, d), lambda i: (i, 0)),
            pl.BlockSpec((tm, a.shape[1]), lambda i: (i, 0)),
            _const_spec(w_o.shape), _const_spec((1, d)), _const_spec((1, d)),
        ],
        out_specs=pl.BlockSpec((tm, d), lambda i: (i, 0)),
        out_shape=jax.ShapeDtypeStruct((t, d), F32),
        compiler_params=_cparams(1),
        name="oproj",
    )(h, a, w_o, g, b)


def _rel_bucket_np(dist):
    n = np.maximum(dist, 0)
    max_exact = REL_BUCKETS // 2
    nf = np.maximum(n, 1).astype(np.float32)
    large = max_exact + (np.log(nf / np.float32(max_exact)) / np.float32(math.log(REL_MAX_DIST / max_exact))
                         * np.float32(REL_BUCKETS - max_exact)).astype(np.int32)
    return np.where(n < max_exact, n, np.minimum(large, REL_BUCKETS - 1)).astype(np.int32)


def _bias_tables(rel_table, seq, tq, n_cmp_pad):
    tbl2 = rel_table * LOG2E
    far = REL_BUCKETS - 1
    c_hi = tbl2[far].astype(BF16)
    c_lo = (tbl2[far] - c_hi.astype(F32)).astype(BF16)
    c2 = c_hi.astype(F32) + c_lo.astype(F32)

    def lookup(buckets, visible, head_axis, offset):
        shape = [1] * buckets.ndim
        shape[head_axis] = N_HEADS
        b = jnp.asarray(np.where(visible, buckets, -1).astype(np.int8))
        out = jnp.full(shape, NEG, F32)
        for bucket in range(REL_BUCKETS):
            out = jnp.where(b == bucket, (tbl2[bucket] - offset).reshape(shape), out)
        return out

    q = np.arange(tq)[:, None]
    k = np.arange(tq)[None, :]
    d0 = q - k
    d1 = q - k + tq
    d2 = q - k + 2 * tq
    assert tq >= REL_MAX_DIST and 2 * tq == WINDOW
    buckets = np.stack([_rel_bucket_np(d0), _rel_bucket_np(d1), _rel_bucket_np(d2)])[None]
    visible = np.stack([d0 >= 0, np.ones((tq, tq), bool), d2 < WINDOW])[None]
    bias_d = lookup(buckets, visible, 0, c2)

    n_qt = seq // tq
    t = (np.arange(n_qt)[:, None, None, None] * tq + np.arange(tq)[None, None, :, None])
    i = np.arange(n_cmp_pad)[None, None, None, :]
    dc = t - (i * CMP_STRIDE + CMP_LEN - 1)
    n_cmp = seq // CMP_STRIDE - CMP_LEN // CMP_STRIDE + 1
    vis_c = np.logical_and(dc >= 0, i < n_cmp)
    bias_c = lookup(_rel_bucket_np(dc), vis_c, 1, 0.0)

    cfeat = jnp.zeros((N_HEADS, 1, LANES - HEAD_DIM), BF16)
    cfeat = cfeat.at[:, 0, BIAS_F0 - HEAD_DIM].set(c_hi).at[:, 0, BIAS_F0 - HEAD_DIM + 1].set(c_lo)
    return bias_c, bias_d, cfeat


def _overlap_t(n_cmp_pad, n_sel):
    n_cmp = n_cmp_pad - CMP_LEN // CMP_STRIDE + 1
    i = np.arange(n_cmp_pad)[None, :]
    j = np.arange(n_sel)[:, None]
    ov = (i * CMP_STRIDE < (j + 1) * SEL_LEN) & (i * CMP_STRIDE + CMP_LEN > j * SEL_LEN) & (i < n_cmp)
    return ov.astype(np.float32)


def _gate_expand():
    n_pairs = HEADS_PER_GROUP // 2
    ex = np.zeros((GATE_COLS, N_BRANCH * n_pairs * LANES), np.float32)
    for head in range(HEADS_PER_GROUP):
        pair, odd = divmod(head, 2)
        for br in range(N_BRANCH):
            col = (br * n_pairs + pair) * LANES + odd * HEAD_DIM
            ex[N_BRANCH * head + br, col:col + HEAD_DIM] = 1.0
    return ex


def kernel(x, rel_table, ln_g, ln_b, ffn_w1, ffn_w3, ffn_w2, sgu_w_in, sgu_ln_g, sgu_ln_b, sgu_w_s,
           sgu_b_s, sgu_w_out, kv_w, cmp_pe, cmp_w1, cmp_b1, cmp_w2, nsa_w_qg, nsa_w_o):
    bsz, seq, d = x.shape
    t = bsz * seq
    h = x.reshape(t, d)
    n_chunks = seq // CMP_STRIDE
    tq = min(ATT_Q, seq)

    def ffn(h, layer, half, norm):
        return _ffn(h, ffn_w1[layer, half].astype(BF16), ffn_w3[layer, half].astype(BF16),
                    ffn_w2[layer, half].astype(BF16), ln_g[layer, norm][None], ln_b[layer, norm][None])

    shared = None
    for layer in range(DEPTH):
        if layer == N_A_LAYERS:
            kv_cmp, kv6 = _kv_proj(h.reshape(bsz, seq, d), kv_w.astype(BF16))
            r = kv_cmp.reshape(bsz, n_chunks, CMP_STRIDE, 2, N_KV_GROUPS, HEAD_DIM)
            r = r.transpose(3, 0, 4, 1, 2, 5).reshape(2, bsz, N_KV_GROUPS, n_chunks, CMP_STRIDE * HEAD_DIM)
            zpad = jnp.zeros(cmp_w2.shape[1:], cmp_w2.dtype)
            w2_pad = jnp.stack([jnp.concatenate([cmp_w2[0], zpad], axis=1),
                                jnp.concatenate([cmp_w2[1], zpad], axis=1),
                                jnp.concatenate([zpad, cmp_w2[1]], axis=1)]).astype(BF16)
            kvc = _compress(r, cmp_pe.reshape(2, 1, CMP_LEN * HEAD_DIM), cmp_w1.astype(BF16),
                            cmp_b1[:, None, :], w2_pad)
            bias_c, bias_d, cfeat = _bias_tables(rel_table, seq, tq, n_chunks)
            ov_t = jnp.asarray(_overlap_t(n_chunks, seq // SEL_LEN), BF16)
            shared = (kvc, kv6, bias_c, bias_d, cfeat, ov_t, jnp.asarray(_gate_expand(), BF16))
        h = ffn(h, layer, 0, 0)
        if layer < N_A_LAYERS:
            a = layer
            h = _sgu(h, sgu_w_in[a].astype(BF16), sgu_ln_g[a][None], sgu_ln_b[a][None], sgu_w_s[a],
                     sgu_b_s[a].T, sgu_w_out[a].astype(BF16), ln_g[layer, 1][None], ln_b[layer, 1][None])
        else:
            bl = layer - N_A_LAYERS
            kvc, kv6, bias_c, bias_d, cfeat, ov_t, gate_expand = shared
            nq = N_HEADS * HEAD_DIM
            wq = nsa_w_qg[bl][:, :nq].astype(BF16)
            wg = nsa_w_qg[bl][:, nq:].reshape(d, N_KV_GROUPS, 3 * HEADS_PER_GROUP)
            wg = jnp.pad(wg, ((0, 0), (0, 0), (0, GATE_COLS - 3 * HEADS_PER_GROUP)))
            wg = wg.reshape(d, N_KV_GROUPS * GATE_COLS).astype(BF16)
            q, gates = _qg_proj(h.reshape(bsz, seq, d), wq, wg, cfeat)
            att = _attention(q, kvc, kv6, bias_c, bias_d, gates, ov_t, gate_expand)
            h = _oproj(h, att.reshape(t, nq), nsa_w_o[bl].astype(BF16), ln_g[layer, 1][None], ln_b[layer, 1][None])
        h = ffn(h, layer, 1, 2)
    return h.reshape(bsz, seq, d)
```

```python
import functools
import math

import numpy as np
import jax
import jax.numpy as jnp
from jax import lax
from jax.experimental import pallas as pl
from jax.experimental.pallas import tpu as pltpu

DEPTH = 4
N_A_LAYERS = DEPTH // 2
DN_ALPHA = (2.0 * DEPTH) ** 0.25
LN_EPS = 1e-5
SGU_GROUPS = 8
SGU_CHUNK = 128
N_HEADS = 16
N_KV_GROUPS = 4
HEADS_PER_GROUP = N_HEADS // N_KV_GROUPS
HEAD_DIM = 64
CMP_LEN = 32
CMP_STRIDE = 16
SEL_LEN = 64
SEL_TOP_N = 8
N_LOCAL_BLOCKS = 2
WINDOW = 512
REL_BUCKETS = 32
REL_MAX_DIST = 128
NEG = -1e30
M_INIT = -1e20
LOG2E = 1.0 / math.log(2.0)

BF16 = jnp.bfloat16
F32 = jnp.float32

FFN_ROWS = 512
FFN_COLS = 256
SGU_ROWS = 512
PROJ_ROWS = 512
ATT_Q = 256
ATT_ROWS = 128
RANK_ROWS = 16
LANES = 128
SEL_F0 = HEAD_DIM
BIAS_F0 = 96
MAX_SEL_BLOCKS = BIAS_F0 - SEL_F0
GATE_COLS = 32
HEAD_ORDER = (0, 2, 1, 3)
N_BRANCH = 3
KV_SLOTS = 6
VMEM_LIMIT = 56 * 1024 * 1024


def _cparams(n_axes):
    return pltpu.CompilerParams(
        dimension_semantics=("arbitrary",) * n_axes, vmem_limit_bytes=VMEM_LIMIT)


def _const_spec(shape):
    nd = len(shape)
    return pl.BlockSpec(shape, lambda *_: (0,) * nd)


def _layer_norm(y, g, b):
    mu = jnp.mean(y, axis=-1, keepdims=True)
    yc = y - mu
    var = jnp.mean(yc * yc, axis=-1, keepdims=True)
    return yc * lax.rsqrt(var + LN_EPS) * g + b


def _gelu(x):
    return 0.5 * x * (1.0 + lax.erf(x * math.sqrt(0.5)))


def _dot(a, b):
    return jnp.dot(a, b, preferred_element_type=F32)


def _dot_nt(a, b):
    return lax.dot_general(a, b, (((1,), (1,)), ((), ())), preferred_element_type=F32)


def _ffn_body(h_ref, w1_ref, w3_ref, w2_ref, g_ref, b_ref, o_ref, hid_ref):
    x = h_ref[...]
    xb = x.astype(BF16)
    ffn = w1_ref.shape[1]
    for c in range(ffn // FFN_COLS):
        cs = slice(c * FFN_COLS, (c + 1) * FFN_COLS)
        a = _dot(xb, w1_ref[:, cs])
        b3 = _dot(xb, w3_ref[:, cs])
        hid_ref[:, cs] = (jax.nn.silu(a) * b3).astype(BF16)
    f = _dot(hid_ref[...], w2_ref[...])
    o_ref[...] = _layer_norm(DN_ALPHA * x + 0.5 * f, g_ref[...], b_ref[...])


def _ffn(h, w1, w3, w2, g, b):
    t, d = h.shape
    ffn = w1.shape[1]
    tm = min(FFN_ROWS, t)
    return pl.pallas_call(
        _ffn_body,
        grid=(t // tm,),
        in_specs=[
            pl.BlockSpec((tm, d), lambda i: (i, 0)),
            _const_spec((d, ffn)), _const_spec((d, ffn)), _const_spec((ffn, d)),
            _const_spec((1, d)), _const_spec((1, d)),
        ],
        out_specs=pl.BlockSpec((tm, d), lambda i: (i, 0)),
        out_shape=jax.ShapeDtypeStruct((t, d), F32),
        scratch_shapes=[pltpu.VMEM((tm, ffn), BF16)],
        compiler_params=_cparams(1),
        name="ffn",
    )(h, w1, w3, w2, g, b)


def _sgu_body(h_ref, win_ref, lg_ref, lb_ref, ws_ref, bs_ref, wout_ref, g_ref, b_ref,
              o_ref, v_ref, gat_ref):
    x = h_ref[...]
    xb = x.astype(BF16)
    tm = x.shape[0]
    half = wout_ref.shape[0]
    gd = half // SGU_GROUPS
    for c in range(SGU_GROUPS):
        cs = slice(c * gd, (c + 1) * gd)
        v_ref[:, cs] = _gelu(_dot(xb, win_ref[:, half + c * gd: half + (c + 1) * gd]))
    v = v_ref[...]
    mu = jnp.mean(v, axis=-1, keepdims=True)
    var = jnp.mean((v - mu) * (v - mu), axis=-1, keepdims=True)
    rstd = lax.rsqrt(var + LN_EPS)
    row = lax.broadcasted_iota(jnp.int32, (SGU_CHUNK, SGU_CHUNK), 0)
    col = lax.broadcasted_iota(jnp.int32, (SGU_CHUNK, SGU_CHUNK), 1)
    causal = col <= row
    for c in range(SGU_GROUPS):
        cs = slice(c * gd, (c + 1) * gd)
        vn = ((v_ref[:, cs] - mu) * rstd * lg_ref[:, cs] + lb_ref[:, cs]).astype(BF16)
        u = _gelu(_dot(xb, win_ref[:, cs]))
        w = jnp.where(causal, ws_ref[c], 0.0).astype(BF16)
        bias = bs_ref[:, c:c + 1]
        for r in range(tm // SGU_CHUNK):
            rs = slice(r * SGU_CHUNK, (r + 1) * SGU_CHUNK)
            mixed = _dot(w, vn[rs]) + bias
            gat_ref[rs, cs] = (u[rs] * mixed).astype(BF16)
    mix = _dot(gat_ref[...], wout_ref[...])
    o_ref[...] = _layer_norm(DN_ALPHA * x + mix, g_ref[...], b_ref[...])


def _sgu(h, w_in, ln_g, ln_b, w_s, b_s_t, w_out, g, b):
    t, d = h.shape
    hidden = w_in.shape[1]
    half = hidden // 2
    tm = min(SGU_ROWS, t)
    return pl.pallas_call(
        _sgu_body,
        grid=(t // tm,),
        in_specs=[
            pl.BlockSpec((tm, d), lambda i: (i, 0)),
            _const_spec((d, hidden)), _const_spec((1, half)), _const_spec((1, half)),
            _const_spec(w_s.shape), _const_spec(b_s_t.shape), _const_spec((half, d)),
            _const_spec((1, d)), _const_spec((1, d)),
        ],
        out_specs=pl.BlockSpec((tm, d), lambda i: (i, 0)),
        out_shape=jax.ShapeDtypeStruct((t, d), F32),
        scratch_shapes=[pltpu.VMEM((tm, half), F32), pltpu.VMEM((tm, half), BF16)],
        compiler_params=_cparams(1),
        name="sgu",
    )(h, w_in, ln_g, ln_b, w_s, b_s_t, w_out, g, b)


def _kv_body(h_ref, w_ref, cmp_ref, kv_ref):
    xb = h_ref[0].astype(BF16)
    ts = xb.shape[0]
    n_cmp_cols = cmp_ref.shape[-1]
    cmp_ref[0] = _dot(xb, w_ref[:, :n_cmp_cols])
    gw = N_KV_GROUPS * HEAD_DIM
    pad = LANES - HEAD_DIM
    pos = pl.program_id(1) * ts + lax.broadcasted_iota(jnp.int32, (ts, pad), 0)
    lane = lax.broadcasted_iota(jnp.int32, (ts, pad), 1) + HEAD_DIM
    bias_lanes = jnp.logical_and(lane >= BIAS_F0, lane < BIAS_F0 + 2)
    blk_lane = jnp.logical_and(lane < BIAS_F0, pos // SEL_LEN == lane - SEL_F0)
    one = lambda m: jnp.where(m, 1.0, 0.0).astype(BF16)
    k_ext = (one(jnp.logical_or(bias_lanes, blk_lane)), one(bias_lanes))
    ones = jnp.ones((ts, pad), BF16)
    for s in range(4):
        y = _dot(xb, w_ref[:, n_cmp_cols + s * gw: n_cmp_cols + (s + 1) * gw]).astype(BF16)
        out = 3 * (s // 2)
        for g in range(N_KV_GROUPS):
            piece = y[:, g * HEAD_DIM:(g + 1) * HEAD_DIM]
            if s % 2 == 0:
                kv_ref[out, 0, g, :, :HEAD_DIM] = piece
                kv_ref[out, 0, g, :, HEAD_DIM:] = k_ext[s // 2]
            else:
                kv_ref[out + 1, 0, g, :, :HEAD_DIM] = piece
                kv_ref[out + 1, 0, g, :, HEAD_DIM:] = ones
                kv_ref[out + 2, 0, g, :, :HEAD_DIM] = ones
                kv_ref[out + 2, 0, g, :, HEAD_DIM:] = piece


def _kv_proj(h3, kv_w):
    bsz, seq, d = h3.shape
    gw = N_KV_GROUPS * HEAD_DIM
    ts = min(PROJ_ROWS, seq)
    return pl.pallas_call(
        _kv_body,
        grid=(bsz, seq // ts),
        in_specs=[pl.BlockSpec((1, ts, d), lambda b, i: (b, i, 0)), _const_spec(kv_w.shape)],
        out_specs=[
            pl.BlockSpec((1, ts, 2 * gw), lambda b, i: (b, i, 0)),
            pl.BlockSpec((KV_SLOTS, 1, N_KV_GROUPS, ts, LANES), lambda b, i: (0, b, 0, i, 0)),
        ],
        out_shape=[
            jax.ShapeDtypeStruct((bsz, seq, 2 * gw), F32),
            jax.ShapeDtypeStruct((KV_SLOTS, bsz, N_KV_GROUPS, seq, LANES), BF16),
        ],
        compiler_params=_cparams(2),
        name="kv_proj",
    )(h3, kv_w)


def _cmp_body(r_ref, pe_ref, w1_ref, b1_ref, w2_ref, o_ref):
    n_chunks = r_ref.shape[3]
    rows = N_KV_GROUPS * n_chunks
    half = r_ref.shape[4]
    r = r_ref[0, 0].reshape(rows, half)
    top = (r + pe_ref[0, :, :half]).astype(BF16)
    bot = (r + pe_ref[0, :, half:]).astype(BF16)
    a = _dot(top, w1_ref[0, :half])
    bm = _dot(bot, w1_ref[0, half:])
    pre = a + pltpu.roll(bm, rows - 1, 0) + b1_ref[0]
    hid = _gelu(pre).astype(BF16)
    o_ref[0, 0] = _dot(hid, w2_ref[0]).astype(BF16).reshape(N_KV_GROUPS, n_chunks, LANES)


def _compress(r, pe, w1, b1, w2):
    _, bsz, g, n_chunks, width = r.shape
    phi = w1.shape[-1]
    n_out = w2.shape[0]
    src = lambda s: jnp.minimum(s, 1)
    return pl.pallas_call(
        _cmp_body,
        grid=(n_out, bsz),
        in_specs=[
            pl.BlockSpec((1, 1, g, n_chunks, width), lambda s, b: (src(s), b, 0, 0, 0)),
            pl.BlockSpec((1, 1, 2 * width), lambda s, b: (src(s), 0, 0)),
            pl.BlockSpec((1, 2 * width, phi), lambda s, b: (src(s), 0, 0)),
            pl.BlockSpec((1, 1, phi), lambda s, b: (src(s), 0, 0)),
            pl.BlockSpec((1, phi, LANES), lambda s, b: (s, 0, 0)),
        ],
        out_specs=pl.BlockSpec((1, 1, g, n_chunks, LANES), lambda s, b: (s, b, 0, 0, 0)),
        out_shape=jax.ShapeDtypeStruct((n_out, bsz, g, n_chunks, LANES), BF16),
        compiler_params=_cparams(2),
        name="compress",
    )(r, pe, w1, b1, w2)


def _qg_body(h_ref, wq_ref, wg_ref, cf_ref, q_ref, gate_ref):
    xb = h_ref[0].astype(BF16)
    ts = xb.shape[0]
    q = (_dot(xb, wq_ref[...]) * (HEAD_DIM ** -0.5 * LOG2E)).astype(BF16)
    for slot in range(N_HEADS):
        hd = slot - slot % HEADS_PER_GROUP + HEAD_ORDER[slot % HEADS_PER_GROUP]
        q_ref[0, slot, :, :HEAD_DIM] = q[:, hd * HEAD_DIM:(hd + 1) * HEAD_DIM]
        q_ref[0, slot, :, HEAD_DIM:] = jnp.broadcast_to(cf_ref[hd], (ts, LANES - HEAD_DIM))
    gate = jax.nn.sigmoid(_dot(xb, wg_ref[...]))
    gcols = gate_ref.shape[-1]
    for g in range(N_KV_GROUPS):
        gate_ref[0, g] = gate[:, g * gcols:(g + 1) * gcols]


def _qg_proj(h3, wq, wg, cfeat):
    bsz, seq, d = h3.shape
    ts = min(PROJ_ROWS, seq)
    gcols = wg.shape[1] // N_KV_GROUPS
    return pl.pallas_call(
        _qg_body,
        grid=(bsz, seq // ts),
        in_specs=[pl.BlockSpec((1, ts, d), lambda b, i: (b, i, 0)),
                  _const_spec(wq.shape), _const_spec(wg.shape), _const_spec(cfeat.shape)],
        out_specs=[
            pl.BlockSpec((1, N_HEADS, ts, LANES), lambda b, i: (b, 0, i, 0)),
            pl.BlockSpec((1, N_KV_GROUPS, ts, gcols), lambda b, i: (b, 0, i, 0)),
        ],
        out_shape=[
            jax.ShapeDtypeStruct((bsz, N_HEADS, seq, LANES), BF16),
            jax.ShapeDtypeStruct((bsz, N_KV_GROUPS, seq, gcols), F32),
        ],
        compiler_params=_cparams(2),
        name="qg_proj",
    )(h3, wq, wg, cfeat)


def _attn_body(q_ref, kc_ref, vce_ref, vco_ref, ks_ref, vse_ref, vso_ref, kw_ref, vwe_ref, vwo_ref,
               bc_ref, bd_ref, gate_ref, ov_ref, gx_ref, o_ref,
               ps_ref, pc_ref, cs_ref, oc_ref, qa_ref, sm_ref, sa_ref, wm_ref, wa_ref, *tile_scratch):
    hg, tq = q_ref.shape[1], q_ref.shape[2]
    rows = hg * tq
    half_rows = rows // 2
    n_cmp = kc_ref.shape[2]
    n_sel = ov_ref.shape[0]
    n_qt = ks_ref.shape[2] // tq
    rb_per_head = tq // ATT_ROWS
    halves = (slice(0, half_rows), slice(half_rows, rows))
    sets = [tile_scratch[3 * i:3 * i + 3] for i in range(len(tile_scratch) // 3)]
    sel_sets, win_sets = sets[:2], sets[2:]
    sel_kv = (ks_ref, (vse_ref, vso_ref))
    win_kv = (kw_ref, (vwe_ref, vwo_ref))

    def q_rows(src_ref, hv):
        if src_ref is q_ref:
            return q_ref[0, hv * (hg // 2):(hv + 1) * (hg // 2)].reshape(half_rows, LANES)
        return src_ref[halves[hv], :]

    def row_blocks(hv):
        out = []
        for rb in range(hv * half_rows // ATT_ROWS, (hv + 1) * half_rows // ATT_ROWS):
            slot, part = divmod(rb, rb_per_head)
            out.append((slice(rb * ATT_ROWS, (rb + 1) * ATT_ROWS), HEAD_ORDER[slot],
                        slice(part * ATT_ROWS, (part + 1) * ATT_ROWS)))
        return out

    def tile(src_ref, k_ref, v_refs, start, slot, scratch, m_ref, a_ref):
        s_ref, p_ref, l_ref = scratch
        k = k_ref[0, 0, start:start + tq, :]
        for hv in range(2):
            hs = halves[hv]
            s_ref[hs, :] = _dot_nt(q_rows(src_ref, hv), k)
            for rs, head, qs in row_blocks(hv):
                pieces = []
                for c in range(tq // LANES):
                    cs = slice(c * LANES, (c + 1) * LANES)
                    sc = s_ref[rs, cs]
                    if slot is not None:
                        sc = sc + bd_ref[head, slot, qs, cs]
                    pieces.append(sc)
                mx = pieces[0]
                for sc in pieces[1:]:
                    mx = jnp.maximum(mx, sc)
                m_old = m_ref[rs, :]
                m_new = jnp.maximum(m_old, jnp.max(mx, axis=-1, keepdims=True))
                l_ref[rs, :] = jnp.exp2(m_old - m_new)
                m_ref[rs, :] = m_new
                for c, sc in enumerate(pieces):
                    p_ref[rs, c * LANES:(c + 1) * LANES] = jnp.exp2(sc - m_new).astype(BF16)
            pv = _dot(p_ref[hs, :], v_refs[hv][0, 0, start:start + tq, :])
            a_ref[hs, :] = l_ref[hs, :] * a_ref[hs, :] + pv

    def step(qt):
        q0 = qt * tq
        for m_ref, a_ref in ((sm_ref, sa_ref), (wm_ref, wa_ref)):
            m_ref[...] = jnp.full(m_ref.shape, M_INIT, F32)
            a_ref[...] = jnp.zeros(a_ref.shape, F32)

        kc = kc_ref[0, 0]
        for hv, vc_ref in enumerate((vce_ref, vco_ref)):
            cs_ref[halves[hv], :] = _dot_nt(q_rows(q_ref, hv), kc)
            for rs, head, qs in row_blocks(hv):
                s = cs_ref[rs, :] + bc_ref[0, head, qs, :]
                m = jnp.maximum(jnp.max(s, axis=-1, keepdims=True), M_INIT)
                e = jnp.exp2(s - m)
                l = jnp.sum(e, axis=-1, keepdims=True)
                p = e / jnp.where(l > 0.0, l, 1.0)
                pc_ref[rs, :] = p.astype(BF16)
                if head == 0:
                    ps_ref[qs, :] = p
                else:
                    ps_ref[qs, :] += p
            oc_ref[halves[hv], :] = _dot(pc_ref[halves[hv], :], vc_ref[0, 0])

        for d in range(min(qt, WINDOW // tq) + 1):
            tile(q_ref, *win_kv, (qt - d) * tq, d, win_sets[d % 2], wm_ref, wa_ref)

        psum = ps_ref[...]
        p_hi = psum.astype(BF16)
        r1 = psum - p_hi.astype(F32)
        p_mid = r1.astype(BF16)
        p_lo = (r1 - p_mid.astype(F32)).astype(BF16)
        n_blk = min(n_sel, -(-(q0 + tq) // (SEL_LEN * RANK_ROWS)) * RANK_ROWS)
        ov = ov_ref[:n_blk, :]
        imp = _dot_nt(ov, p_hi) + _dot_nt(ov, p_mid) + _dot_nt(ov, p_lo)
        blk = lax.broadcasted_iota(jnp.int32, (n_blk, tq), 0)
        cur = (q0 + lax.broadcasted_iota(jnp.int32, (n_blk, tq), 1)) // SEL_LEN
        valid = blk <= cur
        forced = jnp.logical_and(valid, jnp.logical_or(blk == 0, blk > cur - N_LOCAL_BLOCKS))
        val = jnp.where(forced, -NEG, jnp.where(valid, imp, NEG))
        rank = jnp.zeros((n_blk, tq), F32)
        for j in range(n_blk):
            rj = val[j:j + 1, :]
            ahead = jnp.logical_or(rj > val, jnp.logical_and(rj == val, blk > j))
            rank = rank + jnp.where(ahead, 1.0, 0.0)
        feat = jnp.where(rank < float(min(SEL_TOP_N, n_sel)), 0.0, NEG)
        feat = jnp.concatenate([jnp.zeros((SEL_F0, tq), F32), feat,
                                jnp.zeros((LANES - SEL_F0 - n_blk, tq), F32)], axis=0)
        feat_t = feat.T
        for slot in range(hg):
            qa_ref[slot * tq:(slot + 1) * tq, :] = (q_ref[0, slot].astype(F32) + feat_t).astype(BF16)

        for i, p in enumerate(range(qt, -1, -1)):
            back = qt - p
            tile(qa_ref, *sel_kv, p * tq, back if back < 2 else None, sel_sets[i % 2], sm_ref, sa_ref)

        gates = gate_ref[0, 0]
        g_hi = gates.astype(BF16)
        g_lo = (gates - g_hi.astype(F32)).astype(BF16)
        gx = _dot(g_hi, gx_ref[...]) + _dot(g_lo, gx_ref[...])
        low = lax.broadcasted_iota(jnp.int32, (tq, LANES), 1) < HEAD_DIM
        n_pairs = hg // 2
        for j in range(n_pairs):
            even = slice(j * tq, (j + 1) * tq)
            odd = slice(half_rows + j * tq, half_rows + (j + 1) * tq)
            out = gx[:, j * LANES:(j + 1) * LANES] * (oc_ref[even, :] + oc_ref[odd, :])
            for br, a_ref in ((1, sa_ref), (2, wa_ref)):
                a_e = a_ref[even, :]
                a_o = a_ref[odd, :]
                num = jnp.where(low, a_e, a_o)
                den = pltpu.roll(jnp.where(low, a_o, a_e), HEAD_DIM, 1)
                col = (br * n_pairs + j) * LANES
                out = out + gx[:, col:col + LANES] * (num / jnp.where(den > 0.0, den, 1.0))
            o_ref[0, :, j * LANES:(j + 1) * LANES] = out.astype(BF16)

    for c in range(n_qt):
        pl.when(pl.program_id(2) == c)(functools.partial(step, c))


def _attention(q, kvc, kv6, bias_c, bias_d, gates, ov_t, gate_expand):
    bsz, _, seq, _ = q.shape
    tq = min(ATT_Q, seq)
    hg = HEADS_PER_GROUP
    rows = hg * tq
    n_cmp = kvc.shape[3]
    assert ov_t.shape[0] <= MAX_SEL_BLOCKS and n_cmp <= tq and tq % LANES == 0
    kv_spec = lambda slot: pl.BlockSpec(
        (1, 1, seq, LANES), lambda b, g, i, slot=slot: (slot * bsz + b, g, 0, 0))
    kv_flat = kv6.reshape(KV_SLOTS * bsz, N_KV_GROUPS, seq, LANES)
    cmp_spec = lambda slot: pl.BlockSpec(
        (1, 1, n_cmp, LANES), lambda b, g, i, slot=slot: (slot * bsz + b, g, 0, 0))
    cmp_flat = kvc.reshape(3 * bsz, N_KV_GROUPS, n_cmp, LANES)
    stats = [pltpu.VMEM((rows, LANES), F32)] * 4
    tile_set = [pltpu.VMEM((rows, tq), F32), pltpu.VMEM((rows, tq), BF16), pltpu.VMEM((rows, LANES), F32)]
    return pl.pallas_call(
        _attn_body,
        grid=(bsz, N_KV_GROUPS, seq // tq),
        in_specs=[
            pl.BlockSpec((1, hg, tq, LANES), lambda b, g, i: (b, g, i, 0)),
            cmp_spec(0), cmp_spec(1), cmp_spec(2),
            kv_spec(0), kv_spec(1), kv_spec(2), kv_spec(3), kv_spec(4), kv_spec(5),
            pl.BlockSpec((1, hg, tq, n_cmp), lambda b, g, i: (i, g, 0, 0)),
            pl.BlockSpec((hg, 3, tq, tq), lambda b, g, i: (g, 0, 0, 0)),
            pl.BlockSpec((1, 1, tq, gates.shape[-1]), lambda b, g, i: (b, g, i, 0)),
            _const_spec(ov_t.shape), _const_spec(gate_expand.shape),
        ],
        out_specs=pl.BlockSpec((1, tq, hg * HEAD_DIM), lambda b, g, i: (b, i, g)),
        out_shape=jax.ShapeDtypeStruct((bsz, seq, N_HEADS * HEAD_DIM), BF16),
        scratch_shapes=[
            pltpu.VMEM((tq, n_cmp), F32), pltpu.VMEM((rows, n_cmp), BF16), pltpu.VMEM((rows, n_cmp), F32),
            pltpu.VMEM((rows, LANES), F32), pltpu.VMEM((rows, LANES), BF16),
        ] + stats + tile_set * 4,
        compiler_params=_cparams(3),
        name="nsa_attn",
    )(q, cmp_flat, cmp_flat, cmp_flat, *([kv_flat] * KV_SLOTS), bias_c, bias_d, gates, ov_t, gate_expand)


def _oproj_body(h_ref, a_ref, w_ref, g_ref, b_ref, o_ref):
    mix = _dot(a_ref[...], w_ref[...])
    o_ref[...] = _layer_norm(DN_ALPHA * h_ref[...] + mix, g_ref[...], b_ref[...])


def _oproj(h, a, w_o, g, b):
    t, d = h.shape
    tm = min(PROJ_ROWS, t)
    return pl.pallas_call(
        _oproj_body,
        grid=(t // tm,),
        in_specs=[
            pl.BlockSpec((tm, d), lambda i: (i, 0)),
            pl.BlockSpec((tm, a.shape[1]), lambda i: (i, 0)),
            _const_spec(w_o.shape), _const_spec((1, d)), _const_spec((1, d)),
        ],
        out_specs=pl.BlockSpec((tm, d), lambda i: (i, 0)),
        out_shape=jax.ShapeDtypeStruct((t, d), F32),
        compiler_params=_cparams(1),
        name="oproj",
    )(h, a, w_o, g, b)


def _rel_bucket_np(dist):
    n = np.maximum(dist, 0)
    max_exact = REL_BUCKETS // 2
    nf = np.maximum(n, 1).astype(np.float32)
    large = max_exact + (np.log(nf / np.float32(max_exact)) / np.float32(math.log(REL_MAX_DIST / max_exact))
                         * np.float32(REL_BUCKETS - max_exact)).astype(np.int32)
    return np.where(n < max_exact, n, np.minimum(large, REL_BUCKETS - 1)).astype(np.int32)


def _bias_tables(rel_table, seq, tq, n_cmp_pad):
    tbl2 = rel_table * LOG2E
    far = REL_BUCKETS - 1
    c_hi = tbl2[far].astype(BF16)
    c_lo = (tbl2[far] - c_hi.astype(F32)).astype(BF16)
    c2 = c_hi.astype(F32) + c_lo.astype(F32)

    def lookup(buckets, visible, head_axis, offset):
        shape = [1] * buckets.ndim
        shape[head_axis] = N_HEADS
        b = jnp.asarray(np.where(visible, buckets, -1).astype(np.int8))
        out = jnp.full(shape, NEG, F32)
        for bucket in range(REL_BUCKETS):
            out = jnp.where(b == bucket, (tbl2[bucket] - offset).reshape(shape), out)
        return out

    q = np.arange(tq)[:, None]
    k = np.arange(tq)[None, :]
    d0 = q - k
    d1 = q - k + tq
    d2 = q - k + 2 * tq
    assert tq >= REL_MAX_DIST and 2 * tq == WINDOW
    buckets = np.stack([_rel_bucket_np(d0), _rel_bucket_np(d1), _rel_bucket_np(d2)])[None]
    visible = np.stack([d0 >= 0, np.ones((tq, tq), bool), d2 < WINDOW])[None]
    bias_d = lookup(buckets, visible, 0, c2)

    n_qt = seq // tq
    t = (np.arange(n_qt)[:, None, None, None] * tq + np.arange(tq)[None, None, :, None])
    i = np.arange(n_cmp_pad)[None, None, None, :]
    dc = t - (i * CMP_STRIDE + CMP_LEN - 1)
    n_cmp = seq // CMP_STRIDE - CMP_LEN // CMP_STRIDE + 1
    vis_c = np.logical_and(dc >= 0, i < n_cmp)
    bias_c = lookup(_rel_bucket_np(dc), vis_c, 1, 0.0)

    cfeat = jnp.zeros((N_HEADS, 1, LANES - HEAD_DIM), BF16)
    cfeat = cfeat.at[:, 0, BIAS_F0 - HEAD_DIM].set(c_hi).at[:, 0, BIAS_F0 - HEAD_DIM + 1].set(c_lo)
    return bias_c, bias_d, cfeat


def _overlap_t(n_cmp_pad, n_sel):
    n_cmp = n_cmp_pad - CMP_LEN // CMP_STRIDE + 1
    i = np.arange(n_cmp_pad)[None, :]
    j = np.arange(n_sel)[:, None]
    ov = (i * CMP_STRIDE < (j + 1) * SEL_LEN) & (i * CMP_STRIDE + CMP_LEN > j * SEL_LEN) & (i < n_cmp)
    return ov.astype(np.float32)


def _gate_expand():
    n_pairs = HEADS_PER_GROUP // 2
    ex = np.zeros((GATE_COLS, N_BRANCH * n_pairs * LANES), np.float32)
    for head in range(HEADS_PER_GROUP):
        pair, odd = divmod(head, 2)
        for br in range(N_BRANCH):
            col = (br * n_pairs + pair) * LANES + odd * HEAD_DIM
            ex[N_BRANCH * head + br, col:col + HEAD_DIM] = 1.0
    return ex


def kernel(x, rel_table, ln_g, ln_b, ffn_w1, ffn_w3, ffn_w2, sgu_w_in, sgu_ln_g, sgu_ln_b, sgu_w_s,
           sgu_b_s, sgu_w_out, kv_w, cmp_pe, cmp_w1, cmp_b1, cmp_w2, nsa_w_qg, nsa_w_o):
    bsz, seq, d = x.shape
    t = bsz * seq
    h = x.reshape(t, d)
    n_chunks = seq // CMP_STRIDE
    tq = min(ATT_Q, seq)

    def ffn(h, layer, half, norm):
        return _ffn(h, ffn_w1[layer, half].astype(BF16), ffn_w3[layer, half].astype(BF16),
                    ffn_w2[layer, half].astype(BF16), ln_g[layer, norm][None], ln_b[layer, norm][None])

    shared = None
    for layer in range(DEPTH):
        if layer == N_A_LAYERS:
            kv_cmp, kv6 = _kv_proj(h.reshape(bsz, seq, d), kv_w.astype(BF16))
            r = kv_cmp.reshape(bsz, n_chunks, CMP_STRIDE, 2, N_KV_GROUPS, HEAD_DIM)
            r = r.transpose(3, 0, 4, 1, 2, 5).reshape(2, bsz, N_KV_GROUPS, n_chunks, CMP_STRIDE * HEAD_DIM)
            zpad = jnp.zeros(cmp_w2.shape[1:], cmp_w2.dtype)
            w2_pad = jnp.stack([jnp.concatenate([cmp_w2[0], zpad], axis=1),
                                jnp.concatenate([cmp_w2[1], zpad], axis=1),
                                jnp.concatenate([zpad, cmp_w2[1]], axis=1)]).astype(BF16)
            kvc = _compress(r, cmp_pe.reshape(2, 1, CMP_LEN * HEAD_DIM), cmp_w1.astype(BF16),
                            cmp_b1[:, None, :], w2_pad)
            bias_c, bias_d, cfeat = _bias_tables(rel_table, seq, tq, n_chunks)
            ov_t = jnp.asarray(_overlap_t(n_chunks, seq // SEL_LEN), BF16)
            shared = (kvc, kv6, bias_c, bias_d, cfeat, ov_t, jnp.asarray(_gate_expand(), BF16))
        h = ffn(h, layer, 0, 0)
        if layer < N_A_LAYERS:
            a = layer
            h = _sgu(h, sgu_w_in[a].astype(BF16), sgu_ln_g[a][None], sgu_ln_b[a][None], sgu_w_s[a],
                     sgu_b_s[a].T, sgu_w_out[a].astype(BF16), ln_g[layer, 1][None], ln_b[layer, 1][None])
        else:
            bl = layer - N_A_LAYERS
            kvc, kv6, bias_c, bias_d, cfeat, ov_t, gate_expand = shared
            nq = N_HEADS * HEAD_DIM
            wq = nsa_w_qg[bl][:, :nq].astype(BF16)
            wg = nsa_w_qg[bl][:, nq:].reshape(d, N_KV_GROUPS, 3 * HEADS_PER_GROUP)
            wg = jnp.pad(wg, ((0, 0), (0, 0), (0, GATE_COLS - 3 * HEADS_PER_GROUP)))
            wg = wg.reshape(d, N_KV_GROUPS * GATE_COLS).astype(BF16)
            q, gates = _qg_proj(h.reshape(bsz, seq, d), wq, wg, cfeat)
            att = _attention(q, kvc, kv6, bias_c, bias_d, gates, ov_t, gate_expand)
            h = _oproj(h, att.reshape(t, nq), nsa_w_o[bl].astype(BF16), ln_g[layer, 1][None], ln_b[layer, 1][None])
        h = ffn(h, layer, 1, 2)
    return h.reshape(bsz, seq, d)
```

```python
import functools
import math

import numpy as np
import jax
import jax.numpy as jnp
from jax import lax
from jax.experimental import pallas as pl
from jax.experimental.pallas import tpu as pltpu

DEPTH = 4
N_A_LAYERS = DEPTH // 2
DN_ALPHA = (2.0 * DEPTH) ** 0.25
LN_EPS = 1e-5
SGU_GROUPS = 8
SGU_CHUNK = 128
N_HEADS = 16
N_KV_GROUPS = 4
HEADS_PER_GROUP = N_HEADS // N_KV_GROUPS
HEAD_DIM = 64
CMP_LEN = 32
CMP_STRIDE = 16
SEL_LEN = 64
SEL_TOP_N = 8
N_LOCAL_BLOCKS = 2
WINDOW = 512
REL_BUCKETS = 32
REL_MAX_DIST = 128
NEG = -1e30
M_INIT = -1e20
LOG2E = 1.0 / math.log(2.0)

BF16 = jnp.bfloat16
F32 = jnp.float32

FFN_ROWS = 1024
FFN_SUB = 512
FFN_COLS = 256
SGU_ROWS = 512
PROJ_ROWS = 512
ATT_Q = 256
ATT_ROWS = 128
RANK_ROWS = 16
LANES = 128
SEL_F0 = HEAD_DIM
BIAS_F0 = 96
MAX_SEL_BLOCKS = BIAS_F0 - SEL_F0
GATE_COLS = 32
HEAD_ORDER = (0, 2, 1, 3)
N_BRANCH = 3
KV_SLOTS = 6
VMEM_LIMIT = 56 * 1024 * 1024


def _cparams(n_axes):
    return pltpu.CompilerParams(
        dimension_semantics=("arbitrary",) * n_axes, vmem_limit_bytes=VMEM_LIMIT)


def _const_spec(shape):
    nd = len(shape)
    return pl.BlockSpec(shape, lambda *_: (0,) * nd)


def _layer_norm(y, g, b):
    mu = jnp.mean(y, axis=-1, keepdims=True)
    yc = y - mu
    var = jnp.mean(yc * yc, axis=-1, keepdims=True)
    return yc * lax.rsqrt(var + LN_EPS) * g + b


def _gelu(x):
    return 0.5 * x * (1.0 + lax.erf(x * math.sqrt(0.5)))


def _dot(a, b):
    return jnp.dot(a, b, preferred_element_type=F32)


def _dot_nt(a, b):
    return lax.dot_general(a, b, (((1,), (1,)), ((), ())), preferred_element_type=F32)


def _ffn_body(h_ref, w1_ref, w3_ref, w2_ref, g_ref, b_ref, o_ref, hid_ref):
    ffn = w1_ref.shape[1]
    for r in range(h_ref.shape[0] // FFN_SUB):
        rs = slice(r * FFN_SUB, (r + 1) * FFN_SUB)
        x = h_ref[rs, :]
        xb = x.astype(BF16)
        for c in range(ffn // FFN_COLS):
            cs = slice(c * FFN_COLS, (c + 1) * FFN_COLS)
            a = _dot(xb, w1_ref[:, cs])
            b3 = _dot(xb, w3_ref[:, cs])
            hid_ref[rs, cs] = (jax.nn.silu(a) * b3).astype(BF16)
        f = _dot(hid_ref[rs, :], w2_ref[...])
        o_ref[rs, :] = _layer_norm(DN_ALPHA * x + 0.5 * f, g_ref[...], b_ref[...])


def _ffn(h, w1, w3, w2, g, b):
    t, d = h.shape
    ffn = w1.shape[1]
    tm = min(FFN_ROWS, t)
    assert tm % FFN_SUB == 0
    return pl.pallas_call(
        _ffn_body,
        grid=(t // tm,),
        in_specs=[
            pl.BlockSpec((tm, d), lambda i: (i, 0)),
            _const_spec((d, ffn)), _const_spec((d, ffn)), _const_spec((ffn, d)),
            _const_spec((1, d)), _const_spec((1, d)),
        ],
        out_specs=pl.BlockSpec((tm, d), lambda i: (i, 0)),
        out_shape=jax.ShapeDtypeStruct((t, d), F32),
        scratch_shapes=[pltpu.VMEM((tm, ffn), BF16)],
        compiler_params=_cparams(1),
        name="ffn",
    )(h, w1, w3, w2, g, b)


def _sgu_body(h_ref, win_ref, lg_ref, lb_ref, ws_ref, bs_ref, wout_ref, g_ref, b_ref,
              o_ref, v_ref, gat_ref):
    x = h_ref[...]
    xb = x.astype(BF16)
    tm = x.shape[0]
    half = wout_ref.shape[0]
    gd = half // SGU_GROUPS
    for c in range(SGU_GROUPS):
        cs = slice(c * gd, (c + 1) * gd)
        v_ref[:, cs] = _gelu(_dot(xb, win_ref[:, half + c * gd: half + (c + 1) * gd]))
    v = v_ref[...]
    mu = jnp.mean(v, axis=-1, keepdims=True)
    var = jnp.mean((v - mu) * (v - mu), axis=-1, keepdims=True)
    rstd = lax.rsqrt(var + LN_EPS)
    row = lax.broadcasted_iota(jnp.int32, (SGU_CHUNK, SGU_CHUNK), 0)
    col = lax.broadcasted_iota(jnp.int32, (SGU_CHUNK, SGU_CHUNK), 1)
    causal = col <= row
    for c in range(SGU_GROUPS):
        cs = slice(c * gd, (c + 1) * gd)
        vn = ((v_ref[:, cs] - mu) * rstd * lg_ref[:, cs] + lb_ref[:, cs]).astype(BF16)
        u = _gelu(_dot(xb, win_ref[:, cs]))
        w = jnp.where(causal, ws_ref[c], 0.0).astype(BF16)
        bias = bs_ref[:, c:c + 1]
        for r in range(tm // SGU_CHUNK):
            rs = slice(r * SGU_CHUNK, (r + 1) * SGU_CHUNK)
            mixed = _dot(w, vn[rs]) + bias
            gat_ref[rs, cs] = (u[rs] * mixed).astype(BF16)
    mix = _dot(gat_ref[...], wout_ref[...])
    o_ref[...] = _layer_norm(DN_ALPHA * x + mix, g_ref[...], b_ref[...])


def _sgu(h, w_in, ln_g, ln_b, w_s, b_s_t, w_out, g, b):
    t, d = h.shape
    hidden = w_in.shape[1]
    half = hidden // 2
    tm = min(SGU_ROWS, t)
    return pl.pallas_call(
        _sgu_body,
        grid=(t // tm,),
        in_specs=[
            pl.BlockSpec((tm, d), lambda i: (i, 0)),
            _const_spec((d, hidden)), _const_spec((1, half)), _const_spec((1, half)),
            _const_spec(w_s.shape), _const_spec(b_s_t.shape), _const_spec((half, d)),
            _const_spec((1, d)), _const_spec((1, d)),
        ],
        out_specs=pl.BlockSpec((tm, d), lambda i: (i, 0)),
        out_shape=jax.ShapeDtypeStruct((t, d), F32),
        scratch_shapes=[pltpu.VMEM((tm, half), F32), pltpu.VMEM((tm, half), BF16)],
        compiler_params=_cparams(1),
        name="sgu",
    )(h, w_in, ln_g, ln_b, w_s, b_s_t, w_out, g, b)


def _kv_body(h_ref, w_ref, cmp_ref, kv_ref):
    xb = h_ref[0].astype(BF16)
    ts = xb.shape[0]
    n_cmp_cols = cmp_ref.shape[-1]
    cmp_ref[0] = _dot(xb, w_ref[:, :n_cmp_cols])
    gw = N_KV_GROUPS * HEAD_DIM
    pad = LANES - HEAD_DIM
    pos = pl.program_id(1) * ts + lax.broadcasted_iota(jnp.int32, (ts, pad), 0)
    lane = lax.broadcasted_iota(jnp.int32, (ts, pad), 1) + HEAD_DIM
    bias_lanes = jnp.logical_and(lane >= BIAS_F0, lane < BIAS_F0 + 2)
    blk_lane = jnp.logical_and(lane < BIAS_F0, pos // SEL_LEN == lane - SEL_F0)
    one = lambda m: jnp.where(m, 1.0, 0.0).astype(BF16)
    k_ext = (one(jnp.logical_or(bias_lanes, blk_lane)), one(bias_lanes))
    ones = jnp.ones((ts, pad), BF16)
    for s in range(4):
        y = _dot(xb, w_ref[:, n_cmp_cols + s * gw: n_cmp_cols + (s + 1) * gw]).astype(BF16)
        out = 3 * (s // 2)
        for g in range(N_KV_GROUPS):
            piece = y[:, g * HEAD_DIM:(g + 1) * HEAD_DIM]
            if s % 2 == 0:
                kv_ref[out, 0, g, :, :HEAD_DIM] = piece
                kv_ref[out, 0, g, :, HEAD_DIM:] = k_ext[s // 2]
            else:
                kv_ref[out + 1, 0, g, :, :HEAD_DIM] = piece
                kv_ref[out + 1, 0, g, :, HEAD_DIM:] = ones
                kv_ref[out + 2, 0, g, :, :HEAD_DIM] = ones
                kv_ref[out + 2, 0, g, :, HEAD_DIM:] = piece


def _kv_proj(h3, kv_w):
    bsz, seq, d = h3.shape
    gw = N_KV_GROUPS * HEAD_DIM
    ts = min(PROJ_ROWS, seq)
    return pl.pallas_call(
        _kv_body,
        grid=(bsz, seq // ts),
        in_specs=[pl.BlockSpec((1, ts, d), lambda b, i: (b, i, 0)), _const_spec(kv_w.shape)],
        out_specs=[
            pl.BlockSpec((1, ts, 2 * gw), lambda b, i: (b, i, 0)),
            pl.BlockSpec((KV_SLOTS, 1, N_KV_GROUPS, ts, LANES), lambda b, i: (0, b, 0, i, 0)),
        ],
        out_shape=[
            jax.ShapeDtypeStruct((bsz, seq, 2 * gw), F32),
            jax.ShapeDtypeStruct((KV_SLOTS, bsz, N_KV_GROUPS, seq, LANES), BF16),
        ],
        compiler_params=_cparams(2),
        name="kv_proj",
    )(h3, kv_w)


def _cmp_body(r_ref, pe_ref, w1_ref, b1_ref, w2_ref, o_ref):
    n_chunks = r_ref.shape[3]
    rows = N_KV_GROUPS * n_chunks
    half = r_ref.shape[4]
    r = r_ref[0, 0].reshape(rows, half)
    top = (r + pe_ref[0, :, :half]).astype(BF16)
    bot = (r + pe_ref[0, :, half:]).astype(BF16)
    a = _dot(top, w1_ref[0, :half])
    bm = _dot(bot, w1_ref[0, half:])
    pre = a + pltpu.roll(bm, rows - 1, 0) + b1_ref[0]
    hid = _gelu(pre).astype(BF16)
    o_ref[0, 0] = _dot(hid, w2_ref[0]).astype(BF16).reshape(N_KV_GROUPS, n_chunks, LANES)


def _compress(r, pe, w1, b1, w2):
    _, bsz, g, n_chunks, width = r.shape
    phi = w1.shape[-1]
    n_out = w2.shape[0]
    src = lambda s: jnp.minimum(s, 1)
    return pl.pallas_call(
        _cmp_body,
        grid=(n_out, bsz),
        in_specs=[
            pl.BlockSpec((1, 1, g, n_chunks, width), lambda s, b: (src(s), b, 0, 0, 0)),
            pl.BlockSpec((1, 1, 2 * width), lambda s, b: (src(s), 0, 0)),
            pl.BlockSpec((1, 2 * width, phi), lambda s, b: (src(s), 0, 0)),
            pl.BlockSpec((1, 1, phi), lambda s, b: (src(s), 0, 0)),
            pl.BlockSpec((1, phi, LANES), lambda s, b: (s, 0, 0)),
        ],
        out_specs=pl.BlockSpec((1, 1, g, n_chunks, LANES), lambda s, b: (s, b, 0, 0, 0)),
        out_shape=jax.ShapeDtypeStruct((n_out, bsz, g, n_chunks, LANES), BF16),
        compiler_params=_cparams(2),
        name="compress",
    )(r, pe, w1, b1, w2)


def _qg_body(h_ref, wq_ref, wg_ref, cf_ref, q_ref, gate_ref):
    xb = h_ref[0].astype(BF16)
    ts = xb.shape[0]
    q = (_dot(xb, wq_ref[...]) * (HEAD_DIM ** -0.5 * LOG2E)).astype(BF16)
    for slot in range(N_HEADS):
        hd = slot - slot % HEADS_PER_GROUP + HEAD_ORDER[slot % HEADS_PER_GROUP]
        q_ref[0, slot, :, :HEAD_DIM] = q[:, hd * HEAD_DIM:(hd + 1) * HEAD_DIM]
        q_ref[0, slot, :, HEAD_DIM:] = jnp.broadcast_to(cf_ref[hd], (ts, LANES - HEAD_DIM))
    gate = jax.nn.sigmoid(_dot(xb, wg_ref[...]))
    for g in range(N_KV_GROUPS):
        piece = gate[:, g * GATE_COLS:(g + 1) * GATE_COLS]
        gate_ref[0, g, :, :GATE_COLS] = piece
        gate_ref[0, g, :, GATE_COLS:] = piece


def _qg_proj(h3, wq, wg, cfeat):
    bsz, seq, d = h3.shape
    ts = min(PROJ_ROWS, seq)
    gcols = 2 * GATE_COLS
    return pl.pallas_call(
        _qg_body,
        grid=(bsz, seq // ts),
        in_specs=[pl.BlockSpec((1, ts, d), lambda b, i: (b, i, 0)),
                  _const_spec(wq.shape), _const_spec(wg.shape), _const_spec(cfeat.shape)],
        out_specs=[
            pl.BlockSpec((1, N_HEADS, ts, LANES), lambda b, i: (b, 0, i, 0)),
            pl.BlockSpec((1, N_KV_GROUPS, ts, gcols), lambda b, i: (b, 0, i, 0)),
        ],
        out_shape=[
            jax.ShapeDtypeStruct((bsz, N_HEADS, seq, LANES), BF16),
            jax.ShapeDtypeStruct((bsz, N_KV_GROUPS, seq, gcols), F32),
        ],
        compiler_params=_cparams(2),
        name="qg_proj",
    )(h3, wq, wg, cfeat)


def _attn_body(q_ref, kc_ref, vce_ref, vco_ref, ks_ref, vse_ref, vso_ref, kw_ref, vwe_ref, vwo_ref,
               bc_ref, bd_ref, gate_ref, ov_ref, gx_ref, o_ref,
               ps_ref, pc_ref, cs_ref, oc_ref, qa_ref, sm_ref, sa_ref, wm_ref, wa_ref, *tile_scratch):
    hg, tq = q_ref.shape[1], q_ref.shape[2]
    rows = hg * tq
    half_rows = rows // 2
    n_cmp = kc_ref.shape[2]
    n_sel = ov_ref.shape[0]
    n_qt = ks_ref.shape[2] // tq
    rb_per_head = tq // ATT_ROWS
    halves = (slice(0, half_rows), slice(half_rows, rows))
    sets = [tile_scratch[3 * i:3 * i + 3] for i in range(len(tile_scratch) // 3)]
    sel_sets, win_sets = sets[:2], sets[2:]
    sel_kv = (ks_ref, (vse_ref, vso_ref))
    win_kv = (kw_ref, (vwe_ref, vwo_ref))

    def q_rows(src_ref, hv):
        if src_ref is q_ref:
            return q_ref[0, hv * (hg // 2):(hv + 1) * (hg // 2)].reshape(half_rows, LANES)
        return src_ref[halves[hv], :]

    def row_blocks(hv):
        out = []
        for rb in range(hv * half_rows // ATT_ROWS, (hv + 1) * half_rows // ATT_ROWS):
            slot, part = divmod(rb, rb_per_head)
            out.append((slice(rb * ATT_ROWS, (rb + 1) * ATT_ROWS), HEAD_ORDER[slot],
                        slice(part * ATT_ROWS, (part + 1) * ATT_ROWS)))
        return out

    def tile(src_ref, k_ref, v_refs, start, slot, scratch, m_ref, a_ref, have_scores=False):
        s_ref, p_ref, l_ref = scratch
        k = k_ref[0, 0, start:start + tq, :]
        for hv in range(2):
            hs = halves[hv]
            if not have_scores:
                s_ref[hs, :] = _dot_nt(q_rows(src_ref, hv), k)
            for rs, head, qs in row_blocks(hv):
                pieces = []
                for c in range(tq // LANES):
                    cs = slice(c * LANES, (c + 1) * LANES)
                    sc = s_ref[rs, cs]
                    if slot is not None:
                        sc = sc + bd_ref[head, slot, qs, cs]
                    pieces.append(sc)
                mx = pieces[0]
                for sc in pieces[1:]:
                    mx = jnp.maximum(mx, sc)
                m_old = m_ref[rs, :]
                m_new = jnp.maximum(m_old, jnp.max(mx, axis=-1, keepdims=True))
                l_ref[rs, :] = jnp.exp2(m_old - m_new)
                m_ref[rs, :] = m_new
                for c, sc in enumerate(pieces):
                    p_ref[rs, c * LANES:(c + 1) * LANES] = jnp.exp2(sc - m_new).astype(BF16)
            pv = _dot(p_ref[hs, :], v_refs[hv][0, 0, start:start + tq, :])
            a_ref[hs, :] = l_ref[hs, :] * a_ref[hs, :] + pv

    def step(qt):
        q0 = qt * tq
        for m_ref, a_ref in ((sm_ref, sa_ref), (wm_ref, wa_ref)):
            m_ref[...] = jnp.full(m_ref.shape, M_INIT, F32)
            a_ref[...] = jnp.zeros(a_ref.shape, F32)

        k_both = jnp.concatenate([kc_ref[0, 0], kw_ref[0, 0, q0:q0 + tq, :]], axis=0)
        for hv, vc_ref in enumerate((vce_ref, vco_ref)):
            both = _dot_nt(q_rows(q_ref, hv), k_both)
            cs_ref[halves[hv], :] = both[:, :n_cmp]
            win_sets[0][0][halves[hv], :] = both[:, n_cmp:]
            for rs, head, qs in row_blocks(hv):
                s = cs_ref[rs, :] + bc_ref[0, head, qs, :]
                m = jnp.maximum(jnp.max(s, axis=-1, keepdims=True), M_INIT)
                e = jnp.exp2(s - m)
                l = jnp.sum(e, axis=-1, keepdims=True)
                p = e / jnp.where(l > 0.0, l, 1.0)
                pc_ref[rs, :] = p.astype(BF16)
                if head == 0:
                    ps_ref[qs, :] = p
                else:
                    ps_ref[qs, :] += p
            oc_ref[halves[hv], :] = _dot(pc_ref[halves[hv], :], vc_ref[0, 0])

        for d in range(min(qt, WINDOW // tq) + 1):
            tile(q_ref, *win_kv, (qt - d) * tq, d, win_sets[d % 2], wm_ref, wa_ref, have_scores=d == 0)

        psum = ps_ref[...]
        p_hi = psum.astype(BF16)
        r1 = psum - p_hi.astype(F32)
        p_mid = r1.astype(BF16)
        p_lo = (r1 - p_mid.astype(F32)).astype(BF16)
        n_blk = min(n_sel, -(-(q0 + tq) // (SEL_LEN * RANK_ROWS)) * RANK_ROWS)
        ov = ov_ref[:n_blk, :]
        imp = _dot_nt(ov, p_hi) + _dot_nt(ov, p_mid) + _dot_nt(ov, p_lo)
        blk = lax.broadcasted_iota(jnp.int32, (n_blk, tq), 0)
        cur = (q0 + lax.broadcasted_iota(jnp.int32, (n_blk, tq), 1)) // SEL_LEN
        valid = blk <= cur
        forced = jnp.logical_and(valid, jnp.logical_or(blk == 0, blk > cur - N_LOCAL_BLOCKS))
        val = jnp.where(forced, -NEG, jnp.where(valid, imp, NEG))
        rank = jnp.zeros((n_blk, tq), F32)
        for j in range(n_blk):
            rj = val[j:j + 1, :]
            ahead = jnp.logical_or(rj > val, jnp.logical_and(rj == val, blk > j))
            rank = rank + jnp.where(ahead, 1.0, 0.0)
        feat = jnp.where(rank < float(min(SEL_TOP_N, n_sel)), 0.0, NEG)
        feat = jnp.concatenate([jnp.zeros((SEL_F0, tq), F32), feat,
                                jnp.zeros((LANES - SEL_F0 - n_blk, tq), F32)], axis=0)
        feat_t = feat.T
        for slot in range(hg):
            qa_ref[slot * tq:(slot + 1) * tq, :] = (q_ref[0, slot].astype(F32) + feat_t).astype(BF16)

        for i, p in enumerate(range(qt, -1, -1)):
            back = qt - p
            tile(qa_ref, *sel_kv, p * tq, back if back < 2 else None, sel_sets[i % 2], sm_ref, sa_ref)

        gates = gate_ref[0, 0]
        g_hi = gates.astype(BF16)
        g_lo = (gates - g_hi.astype(F32)).astype(BF16)
        first = lax.broadcasted_iota(jnp.int32, gates.shape, 1) < GATE_COLS
        gx = _dot(jnp.where(first, g_hi, g_lo), gx_ref[...])
        low = lax.broadcasted_iota(jnp.int32, (tq, LANES), 1) < HEAD_DIM
        n_pairs = hg // 2
        for j in range(n_pairs):
            even = slice(j * tq, (j + 1) * tq)
            odd = slice(half_rows + j * tq, half_rows + (j + 1) * tq)
            out = gx[:, j * LANES:(j + 1) * LANES] * (oc_ref[even, :] + oc_ref[odd, :])
            for br, a_ref in ((1, sa_ref), (2, wa_ref)):
                a_e = a_ref[even, :]
                a_o = a_ref[odd, :]
                num = jnp.where(low, a_e, a_o)
                den = pltpu.roll(jnp.where(low, a_o, a_e), HEAD_DIM, 1)
                col = (br * n_pairs + j) * LANES
                out = out + gx[:, col:col + LANES] * (num / jnp.where(den > 0.0, den, 1.0))
            o_ref[0, :, j * LANES:(j + 1) * LANES] = out.astype(BF16)

    for c in range(n_qt):
        pl.when(pl.program_id(2) == c)(functools.partial(step, c))


def _attention(q, kvc, kv6, bias_c, bias_d, gates, ov_t, gate_expand):
    bsz, _, seq, _ = q.shape
    tq = min(ATT_Q, seq)
    hg = HEADS_PER_GROUP
    rows = hg * tq
    n_cmp = kvc.shape[3]
    assert ov_t.shape[0] <= MAX_SEL_BLOCKS and n_cmp <= tq and tq % LANES == 0
    kv_spec = lambda slot: pl.BlockSpec(
        (1, 1, seq, LANES), lambda b, g, i, slot=slot: (slot * bsz + b, g, 0, 0))
    kv_flat = kv6.reshape(KV_SLOTS * bsz, N_KV_GROUPS, seq, LANES)
    cmp_spec = lambda slot: pl.BlockSpec(
        (1, 1, n_cmp, LANES), lambda b, g, i, slot=slot: (slot * bsz + b, g, 0, 0))
    cmp_flat = kvc.reshape(3 * bsz, N_KV_GROUPS, n_cmp, LANES)
    stats = [pltpu.VMEM((rows, LANES), F32)] * 4
    tile_set = [pltpu.VMEM((rows, tq), F32), pltpu.VMEM((rows, tq), BF16), pltpu.VMEM((rows, LANES), F32)]
    return pl.pallas_call(
        _attn_body,
        grid=(bsz, N_KV_GROUPS, seq // tq),
        in_specs=[
            pl.BlockSpec((1, hg, tq, LANES), lambda b, g, i: (b, g, i, 0)),
            cmp_spec(0), cmp_spec(1), cmp_spec(2),
            kv_spec(0), kv_spec(1), kv_spec(2), kv_spec(3), kv_spec(4), kv_spec(5),
            pl.BlockSpec((1, hg, tq, n_cmp), lambda b, g, i: (i, g, 0, 0)),
            pl.BlockSpec((hg, 3, tq, tq), lambda b, g, i: (g, 0, 0, 0)),
            pl.BlockSpec((1, 1, tq, gates.shape[-1]), lambda b, g, i: (b, g, i, 0)),
            _const_spec(ov_t.shape), _const_spec(gate_expand.shape),
        ],
        out_specs=pl.BlockSpec((1, tq, hg * HEAD_DIM), lambda b, g, i: (b, i, g)),
        out_shape=jax.ShapeDtypeStruct((bsz, seq, N_HEADS * HEAD_DIM), BF16),
        scratch_shapes=[
            pltpu.VMEM((tq, n_cmp), F32), pltpu.VMEM((rows, n_cmp), BF16), pltpu.VMEM((rows, n_cmp), F32),
            pltpu.VMEM((rows, LANES), F32), pltpu.VMEM((rows, LANES), BF16),
        ] + stats + tile_set * 4,
        compiler_params=_cparams(3),
        name="nsa_attn",
    )(q, cmp_flat, cmp_flat, cmp_flat, *([kv_flat] * KV_SLOTS), bias_c, bias_d, gates, ov_t, gate_expand)


def _oproj_body(h_ref, a_ref, w_ref, g_ref, b_ref, o_ref):
    mix = _dot(a_ref[...], w_ref[...])
    o_ref[...] = _layer_norm(DN_ALPHA * h_ref[...] + mix, g_ref[...], b_ref[...])


def _oproj(h, a, w_o, g, b):
    t, d = h.shape
    tm = min(PROJ_ROWS, t)
    return pl.pallas_call(
        _oproj_body,
        grid=(t // tm,),
        in_specs=[
            pl.BlockSpec((tm, d), lambda i: (i, 0)),
            pl.BlockSpec((tm, a.shape[1]), lambda i: (i, 0)),
            _const_spec(w_o.shape), _const_spec((1, d)), _const_spec((1, d)),
        ],
        out_specs=pl.BlockSpec((tm, d), lambda i: (i, 0)),
        out_shape=jax.ShapeDtypeStruct((t, d), F32),
        compiler_params=_cparams(1),
        name="oproj",
    )(h, a, w_o, g, b)


def _rel_bucket_np(dist):
    n = np.maximum(dist, 0)
    max_exact = REL_BUCKETS // 2
    nf = np.maximum(n, 1).astype(np.float32)
    large = max_exact + (np.log(nf / np.float32(max_exact)) / np.float32(math.log(REL_MAX_DIST / max_exact))
                         * np.float32(REL_BUCKETS - max_exact)).astype(np.int32)
    return np.where(n < max_exact, n, np.minimum(large, REL_BUCKETS - 1)).astype(np.int32)


def _bias_tables(rel_table, seq, tq, n_cmp_pad):
    tbl2 = rel_table * LOG2E
    far = REL_BUCKETS - 1
    c_hi = tbl2[far].astype(BF16)
    c_lo = (tbl2[far] - c_hi.astype(F32)).astype(BF16)
    c2 = c_hi.astype(F32) + c_lo.astype(F32)

    def lookup(buckets, visible, head_axis, offset):
        shape = [1] * buckets.ndim
        shape[head_axis] = N_HEADS
        b = jnp.asarray(np.where(visible, buckets, -1).astype(np.int8))
        out = jnp.full(shape, NEG, F32)
        for bucket in range(REL_BUCKETS):
            out = jnp.where(b == bucket, (tbl2[bucket] - offset).reshape(shape), out)
        return out

    q = np.arange(tq)[:, None]
    k = np.arange(tq)[None, :]
    d0 = q - k
    d1 = q - k + tq
    d2 = q - k + 2 * tq
    assert tq >= REL_MAX_DIST and 2 * tq == WINDOW
    buckets = np.stack([_rel_bucket_np(d0), _rel_bucket_np(d1), _rel_bucket_np(d2)])[None]
    visible = np.stack([d0 >= 0, np.ones((tq, tq), bool), d2 < WINDOW])[None]
    bias_d = lookup(buckets, visible, 0, c2)

    n_qt = seq // tq
    t = (np.arange(n_qt)[:, None, None, None] * tq + np.arange(tq)[None, None, :, None])
    i = np.arange(n_cmp_pad)[None, None, None, :]
    dc = t - (i * CMP_STRIDE + CMP_LEN - 1)
    n_cmp = seq // CMP_STRIDE - CMP_LEN // CMP_STRIDE + 1
    vis_c = np.logical_and(dc >= 0, i < n_cmp)
    bias_c = lookup(_rel_bucket_np(dc), vis_c, 1, 0.0)

    cfeat = jnp.zeros((N_HEADS, 1, LANES - HEAD_DIM), BF16)
    cfeat = cfeat.at[:, 0, BIAS_F0 - HEAD_DIM].set(c_hi).at[:, 0, BIAS_F0 - HEAD_DIM + 1].set(c_lo)
    return bias_c, bias_d, cfeat


def _overlap_t(n_cmp_pad, n_sel):
    n_cmp = n_cmp_pad - CMP_LEN // CMP_STRIDE + 1
    i = np.arange(n_cmp_pad)[None, :]
    j = np.arange(n_sel)[:, None]
    ov = (i * CMP_STRIDE < (j + 1) * SEL_LEN) & (i * CMP_STRIDE + CMP_LEN > j * SEL_LEN) & (i < n_cmp)
    return ov.astype(np.float32)


def _gate_expand():
    n_pairs = HEADS_PER_GROUP // 2
    ex = np.zeros((GATE_COLS, N_BRANCH * n_pairs * LANES), np.float32)
    for head in range(HEADS_PER_GROUP):
        pair, odd = divmod(head, 2)
        for br in range(N_BRANCH):
            col = (br * n_pairs + pair) * LANES + odd * HEAD_DIM
            ex[N_BRANCH * head + br, col:col + HEAD_DIM] = 1.0
    return np.concatenate([ex, ex], axis=0)


def kernel(x, rel_table, ln_g, ln_b, ffn_w1, ffn_w3, ffn_w2, sgu_w_in, sgu_ln_g, sgu_ln_b, sgu_w_s,
           sgu_b_s, sgu_w_out, kv_w, cmp_pe, cmp_w1, cmp_b1, cmp_w2, nsa_w_qg, nsa_w_o):
    bsz, seq, d = x.shape
    t = bsz * seq
    h = x.reshape(t, d)
    n_chunks = seq // CMP_STRIDE
    tq = min(ATT_Q, seq)

    def ffn(h, layer, half, norm):
        return _ffn(h, ffn_w1[layer, half].astype(BF16), ffn_w3[layer, half].astype(BF16),
                    ffn_w2[layer, half].astype(BF16), ln_g[layer, norm][None], ln_b[layer, norm][None])

    shared = None
    for layer in range(DEPTH):
        if layer == N_A_LAYERS:
            kv_cmp, kv6 = _kv_proj(h.reshape(bsz, seq, d), kv_w.astype(BF16))
            r = kv_cmp.reshape(bsz, n_chunks, CMP_STRIDE, 2, N_KV_GROUPS, HEAD_DIM)
            r = r.transpose(3, 0, 4, 1, 2, 5).reshape(2, bsz, N_KV_GROUPS, n_chunks, CMP_STRIDE * HEAD_DIM)
            zpad = jnp.zeros(cmp_w2.shape[1:], cmp_w2.dtype)
            w2_pad = jnp.stack([jnp.concatenate([cmp_w2[0], zpad], axis=1),
                                jnp.concatenate([cmp_w2[1], zpad], axis=1),
                                jnp.concatenate([zpad, cmp_w2[1]], axis=1)]).astype(BF16)
            kvc = _compress(r, cmp_pe.reshape(2, 1, CMP_LEN * HEAD_DIM), cmp_w1.astype(BF16),
                            cmp_b1[:, None, :], w2_pad)
            bias_c, bias_d, cfeat = _bias_tables(rel_table, seq, tq, n_chunks)
            ov_t = jnp.asarray(_overlap_t(n_chunks, seq // SEL_LEN), BF16)
            shared = (kvc, kv6, bias_c, bias_d, cfeat, ov_t, jnp.asarray(_gate_expand(), BF16))
        h = ffn(h, layer, 0, 0)
        if layer < N_A_LAYERS:
            a = layer
            h = _sgu(h, sgu_w_in[a].astype(BF16), sgu_ln_g[a][None], sgu_ln_b[a][None], sgu_w_s[a],
                     sgu_b_s[a].T, sgu_w_out[a].astype(BF16), ln_g[layer, 1][None], ln_b[layer, 1][None])
        else:
            bl = layer - N_A_LAYERS
            kvc, kv6, bias_c, bias_d, cfeat, ov_t, gate_expand = shared
            nq = N_HEADS * HEAD_DIM
            wq = nsa_w_qg[bl][:, :nq].astype(BF16)
            wg = nsa_w_qg[bl][:, nq:].reshape(d, N_KV_GROUPS, 3 * HEADS_PER_GROUP)
            wg = jnp.pad(wg, ((0, 0), (0, 0), (0, GATE_COLS - 3 * HEADS_PER_GROUP)))
            wg = wg.reshape(d, N_KV_GROUPS * GATE_COLS).astype(BF16)
            q, gates = _qg_proj(h.reshape(bsz, seq, d), wq, wg, cfeat)
            att = _attention(q, kvc, kv6, bias_c, bias_d, gates, ov_t, gate_expand)
            h = _oproj(h, att.reshape(t, nq), nsa_w_o[bl].astype(BF16), ln_g[layer, 1][None], ln_b[layer, 1][None])
        h = ffn(h, layer, 1, 2)
    return h.reshape(bsz, seq, d)
```

```python
import math

import numpy as np
import jax
import jax.numpy as jnp
from jax import lax
from jax.experimental import pallas as pl
from jax.experimental.pallas import tpu as pltpu

DEPTH = 4
N_A_LAYERS = DEPTH // 2
DN_ALPHA = (2.0 * DEPTH) ** 0.25
LN_EPS = 1e-5
SGU_GROUPS = 8
SGU_CHUNK = 128
N_HEADS = 16
N_KV_GROUPS = 4
HEADS_PER_GROUP = N_HEADS // N_KV_GROUPS
HEAD_DIM = 64
CMP_LEN = 32
CMP_STRIDE = 16
SEL_LEN = 64
SEL_TOP_N = 8
N_LOCAL_BLOCKS = 2
WINDOW = 512
REL_BUCKETS = 32
REL_MAX_DIST = 128
NEG = -1e30
M_INIT = -1e20
LOG2E = 1.0 / math.log(2.0)

BF16 = jnp.bfloat16
F32 = jnp.float32

FFN_ROWS = 1024
FFN_SUB = 512
FFN_COLS = 256
SGU_ROWS = 512
PROJ_ROWS = 512
ATT_Q = 256
ATT_ROWS = 128
RANK_ROWS = 16
LANES = 128
SEL_F0 = HEAD_DIM
BIAS_F0 = 96
MAX_SEL_BLOCKS = BIAS_F0 - SEL_F0
GATE_COLS = 32
HEAD_ORDER = (0, 2, 1, 3)
N_BRANCH = 3
KV_SLOTS = 6
VMEM_LIMIT = 56 * 1024 * 1024


def _cparams(n_axes):
    return pltpu.CompilerParams(
        dimension_semantics=("arbitrary",) * n_axes, vmem_limit_bytes=VMEM_LIMIT)


def _const_spec(shape):
    nd = len(shape)
    return pl.BlockSpec(shape, lambda *_: (0,) * nd)


def _layer_norm(y, g, b):
    mu = jnp.mean(y, axis=-1, keepdims=True)
    yc = y - mu
    var = jnp.mean(yc * yc, axis=-1, keepdims=True)
    return yc * lax.rsqrt(var + LN_EPS) * g + b


def _gelu(x):
    return 0.5 * x * (1.0 + lax.erf(x * math.sqrt(0.5)))


def _dot(a, b):
    return jnp.dot(a, b, preferred_element_type=F32)


def _dot_nt(a, b):
    return lax.dot_general(a, b, (((1,), (1,)), ((), ())), preferred_element_type=F32)


def _ffn_body(h_ref, w1_ref, w3_ref, w2_ref, g_ref, b_ref, o_ref, hid_ref):
    ffn = w1_ref.shape[1]
    for r in range(h_ref.shape[0] // FFN_SUB):
        rs = slice(r * FFN_SUB, (r + 1) * FFN_SUB)
        x = h_ref[rs, :]
        xb = x.astype(BF16)
        for c in range(ffn // FFN_COLS):
            cs = slice(c * FFN_COLS, (c + 1) * FFN_COLS)
            a = _dot(xb, w1_ref[:, cs])
            b3 = _dot(xb, w3_ref[:, cs])
            hid_ref[rs, cs] = (jax.nn.silu(a) * b3).astype(BF16)
        f = _dot(hid_ref[rs, :], w2_ref[...])
        o_ref[rs, :] = _layer_norm(DN_ALPHA * x + 0.5 * f, g_ref[...], b_ref[...])


def _ffn(h, w1, w3, w2, g, b):
    t, d = h.shape
    ffn = w1.shape[1]
    tm = min(FFN_ROWS, t)
    assert tm % FFN_SUB == 0
    return pl.pallas_call(
        _ffn_body,
        grid=(t // tm,),
        in_specs=[
            pl.BlockSpec((tm, d), lambda i: (i, 0)),
            _const_spec((d, ffn)), _const_spec((d, ffn)), _const_spec((ffn, d)),
            _const_spec((1, d)), _const_spec((1, d)),
        ],
        out_specs=pl.BlockSpec((tm, d), lambda i: (i, 0)),
        out_shape=jax.ShapeDtypeStruct((t, d), F32),
        scratch_shapes=[pltpu.VMEM((tm, ffn), BF16)],
        compiler_params=_cparams(1),
        name="ffn",
    )(h, w1, w3, w2, g, b)


def _sgu_body(h_ref, win_ref, lg_ref, lb_ref, ws_ref, bs_ref, wout_ref, g_ref, b_ref,
              o_ref, v_ref, gat_ref):
    x = h_ref[...]
    xb = x.astype(BF16)
    tm = x.shape[0]
    half = wout_ref.shape[0]
    gd = half // SGU_GROUPS
    for c in range(SGU_GROUPS):
        cs = slice(c * gd, (c + 1) * gd)
        v_ref[:, cs] = _gelu(_dot(xb, win_ref[:, half + c * gd: half + (c + 1) * gd]))
    v = v_ref[...]
    mu = jnp.mean(v, axis=-1, keepdims=True)
    var = jnp.mean((v - mu) * (v - mu), axis=-1, keepdims=True)
    rstd = lax.rsqrt(var + LN_EPS)
    row = lax.broadcasted_iota(jnp.int32, (SGU_CHUNK, SGU_CHUNK), 0)
    col = lax.broadcasted_iota(jnp.int32, (SGU_CHUNK, SGU_CHUNK), 1)
    causal = col <= row
    for c in range(SGU_GROUPS):
        cs = slice(c * gd, (c + 1) * gd)
        vn = ((v_ref[:, cs] - mu) * rstd * lg_ref[:, cs] + lb_ref[:, cs]).astype(BF16)
        u = _gelu(_dot(xb, win_ref[:, cs]))
        w = jnp.where(causal, ws_ref[c], 0.0).astype(BF16)
        bias = bs_ref[:, c:c + 1]
        for r in range(tm // SGU_CHUNK):
            rs = slice(r * SGU_CHUNK, (r + 1) * SGU_CHUNK)
            mixed = _dot(w, vn[rs]) + bias
            gat_ref[rs, cs] = (u[rs] * mixed).astype(BF16)
    mix = _dot(gat_ref[...], wout_ref[...])
    o_ref[...] = _layer_norm(DN_ALPHA * x + mix, g_ref[...], b_ref[...])


def _sgu(h, w_in, ln_g, ln_b, w_s, b_s_t, w_out, g, b):
    t, d = h.shape
    hidden = w_in.shape[1]
    half = hidden // 2
    tm = min(SGU_ROWS, t)
    return pl.pallas_call(
        _sgu_body,
        grid=(t // tm,),
        in_specs=[
            pl.BlockSpec((tm, d), lambda i: (i, 0)),
            _const_spec((d, hidden)), _const_spec((1, half)), _const_spec((1, half)),
            _const_spec(w_s.shape), _const_spec(b_s_t.shape), _const_spec((half, d)),
            _const_spec((1, d)), _const_spec((1, d)),
        ],
        out_specs=pl.BlockSpec((tm, d), lambda i: (i, 0)),
        out_shape=jax.ShapeDtypeStruct((t, d), F32),
        scratch_shapes=[pltpu.VMEM((tm, half), F32), pltpu.VMEM((tm, half), BF16)],
        compiler_params=_cparams(1),
        name="sgu",
    )(h, w_in, ln_g, ln_b, w_s, b_s_t, w_out, g, b)


def _kv_body(h_ref, w_ref, cmp_ref, kv_ref):
    xb = h_ref[0].astype(BF16)
    ts = xb.shape[0]
    n_cmp_cols = cmp_ref.shape[-1]
    cmp_ref[0] = _dot(xb, w_ref[:, :n_cmp_cols])
    gw = N_KV_GROUPS * HEAD_DIM
    pad = LANES - HEAD_DIM
    pos = pl.program_id(1) * ts + lax.broadcasted_iota(jnp.int32, (ts, pad), 0)
    lane = lax.broadcasted_iota(jnp.int32, (ts, pad), 1) + HEAD_DIM
    bias_lanes = jnp.logical_and(lane >= BIAS_F0, lane < BIAS_F0 + 2)
    blk_lane = jnp.logical_and(lane < BIAS_F0, pos // SEL_LEN == lane - SEL_F0)
    one = lambda m: jnp.where(m, 1.0, 0.0).astype(BF16)
    k_ext = (one(jnp.logical_or(bias_lanes, blk_lane)), one(bias_lanes))
    ones = jnp.ones((ts, pad), BF16)
    for s in range(4):
        y = _dot(xb, w_ref[:, n_cmp_cols + s * gw: n_cmp_cols + (s + 1) * gw]).astype(BF16)
        out = 3 * (s // 2)
        for g in range(N_KV_GROUPS):
            piece = y[:, g * HEAD_DIM:(g + 1) * HEAD_DIM]
            if s % 2 == 0:
                kv_ref[out, 0, g, :, :HEAD_DIM] = piece
                kv_ref[out, 0, g, :, HEAD_DIM:] = k_ext[s // 2]
            else:
                kv_ref[out + 1, 0, g, :, :HEAD_DIM] = piece
                kv_ref[out + 1, 0, g, :, HEAD_DIM:] = ones
                kv_ref[out + 2, 0, g, :, :HEAD_DIM] = ones
                kv_ref[out + 2, 0, g, :, HEAD_DIM:] = piece


def _kv_proj(h3, kv_w):
    bsz, seq, d = h3.shape
    gw = N_KV_GROUPS * HEAD_DIM
    ts = min(PROJ_ROWS, seq)
    return pl.pallas_call(
        _kv_body,
        grid=(bsz, seq // ts),
        in_specs=[pl.BlockSpec((1, ts, d), lambda b, i: (b, i, 0)), _const_spec(kv_w.shape)],
        out_specs=[
            pl.BlockSpec((1, ts, 2 * gw), lambda b, i: (b, i, 0)),
            pl.BlockSpec((KV_SLOTS, 1, N_KV_GROUPS, ts, LANES), lambda b, i: (0, b, 0, i, 0)),
        ],
        out_shape=[
            jax.ShapeDtypeStruct((bsz, seq, 2 * gw), F32),
            jax.ShapeDtypeStruct((KV_SLOTS, bsz, N_KV_GROUPS, seq, LANES), BF16),
        ],
        compiler_params=_cparams(2),
        name="kv_proj",
    )(h3, kv_w)


def _cmp_body(r_ref, pe_ref, w1_ref, b1_ref, w2_ref, o_ref):
    n_chunks = r_ref.shape[3]
    rows = N_KV_GROUPS * n_chunks
    half = r_ref.shape[4]
    r = r_ref[0, 0].reshape(rows, half)
    top = (r + pe_ref[0, :, :half]).astype(BF16)
    bot = (r + pe_ref[0, :, half:]).astype(BF16)
    a = _dot(top, w1_ref[0, :half])
    bm = _dot(bot, w1_ref[0, half:])
    pre = a + pltpu.roll(bm, rows - 1, 0) + b1_ref[0]
    hid = _gelu(pre).astype(BF16)
    o_ref[0, 0] = _dot(hid, w2_ref[0]).astype(BF16).reshape(N_KV_GROUPS, n_chunks, LANES)


def _compress(r, pe, w1, b1, w2):
    _, bsz, g, n_chunks, width = r.shape
    phi = w1.shape[-1]
    n_out = w2.shape[0]
    src = lambda s: jnp.minimum(s, 1)
    return pl.pallas_call(
        _cmp_body,
        grid=(n_out, bsz),
        in_specs=[
            pl.BlockSpec((1, 1, g, n_chunks, width), lambda s, b: (src(s), b, 0, 0, 0)),
            pl.BlockSpec((1, 1, 2 * width), lambda s, b: (src(s), 0, 0)),
            pl.BlockSpec((1, 2 * width, phi), lambda s, b: (src(s), 0, 0)),
            pl.BlockSpec((1, 1, phi), lambda s, b: (src(s), 0, 0)),
            pl.BlockSpec((1, phi, LANES), lambda s, b: (s, 0, 0)),
        ],
        out_specs=pl.BlockSpec((1, 1, g, n_chunks, LANES), lambda s, b: (s, b, 0, 0, 0)),
        out_shape=jax.ShapeDtypeStruct((n_out, bsz, g, n_chunks, LANES), BF16),
        compiler_params=_cparams(2),
        name="compress",
    )(r, pe, w1, b1, w2)


def _qg_body(h_ref, wq_ref, wg_ref, cf_ref, q_ref, gate_ref):
    xb = h_ref[0].astype(BF16)
    ts = xb.shape[0]
    q = (_dot(xb, wq_ref[...]) * (HEAD_DIM ** -0.5 * LOG2E)).astype(BF16)
    for slot in range(N_HEADS):
        hd = slot - slot % HEADS_PER_GROUP + HEAD_ORDER[slot % HEADS_PER_GROUP]
        q_ref[0, slot, :, :HEAD_DIM] = q[:, hd * HEAD_DIM:(hd + 1) * HEAD_DIM]
        q_ref[0, slot, :, HEAD_DIM:] = jnp.broadcast_to(cf_ref[hd], (ts, LANES - HEAD_DIM))
    gate = jax.nn.sigmoid(_dot(xb, wg_ref[...]))
    for g in range(N_KV_GROUPS):
        piece = gate[:, g * GATE_COLS:(g + 1) * GATE_COLS]
        gate_ref[0, g, :, :GATE_COLS] = piece
        gate_ref[0, g, :, GATE_COLS:] = piece


def _qg_proj(h3, wq, wg, cfeat):
    bsz, seq, d = h3.shape
    ts = min(PROJ_ROWS, seq)
    gcols = 2 * GATE_COLS
    return pl.pallas_call(
        _qg_body,
        grid=(bsz, seq // ts),
        in_specs=[pl.BlockSpec((1, ts, d), lambda b, i: (b, i, 0)),
                  _const_spec(wq.shape), _const_spec(wg.shape), _const_spec(cfeat.shape)],
        out_specs=[
            pl.BlockSpec((1, N_HEADS, ts, LANES), lambda b, i: (b, 0, i, 0)),
            pl.BlockSpec((1, N_KV_GROUPS, ts, gcols), lambda b, i: (b, 0, i, 0)),
        ],
        out_shape=[
            jax.ShapeDtypeStruct((bsz, N_HEADS, seq, LANES), BF16),
            jax.ShapeDtypeStruct((bsz, N_KV_GROUPS, seq, gcols), F32),
        ],
        compiler_params=_cparams(2),
        name="qg_proj",
    )(h3, wq, wg, cfeat)


def _attn_body(q_ref, kc_ref, vce_ref, vco_ref, ks_ref, vse_ref, vso_ref, kw_ref, vwe_ref, vwo_ref,
               bc_ref, bd_ref, gate_ref, ov_ref, gx_ref, o_ref,
               ps_ref, pc_ref, cs_ref, oc_ref, qa_ref, sm_ref, sa_ref, wm_ref, wa_ref, *tile_scratch):
    hg, tq = q_ref.shape[1], bc_ref.shape[2]
    rows = hg * tq
    half_rows = rows // 2
    n_cmp = kc_ref.shape[2]
    n_sel = ov_ref.shape[0]
    n_qt = q_ref.shape[2] // tq
    rb_per_head = tq // ATT_ROWS
    halves = (slice(0, half_rows), slice(half_rows, rows))
    sets = [tile_scratch[3 * i:3 * i + 3] for i in range(len(tile_scratch) // 3)]
    sel_sets, win_sets = sets[:2], sets[2:]
    sel_kv = (ks_ref, (vse_ref, vso_ref))
    win_kv = (kw_ref, (vwe_ref, vwo_ref))

    def q_rows(src, hv):
        if isinstance(src, int):
            return q_ref[0, hv * (hg // 2):(hv + 1) * (hg // 2), src:src + tq, :].reshape(half_rows, LANES)
        return src[halves[hv], :]

    def row_blocks(hv):
        out = []
        for rb in range(hv * half_rows // ATT_ROWS, (hv + 1) * half_rows // ATT_ROWS):
            slot, part = divmod(rb, rb_per_head)
            out.append((slice(rb * ATT_ROWS, (rb + 1) * ATT_ROWS), HEAD_ORDER[slot],
                        slice(part * ATT_ROWS, (part + 1) * ATT_ROWS)))
        return out

    def tile(src_ref, k_ref, v_refs, start, slot, scratch, m_ref, a_ref, have_scores=False):
        s_ref, p_ref, l_ref = scratch
        k = k_ref[0, 0, start:start + tq, :]
        for hv in range(2):
            hs = halves[hv]
            if not have_scores:
                s_ref[hs, :] = _dot_nt(q_rows(src_ref, hv), k)
            for rs, head, qs in row_blocks(hv):
                pieces = []
                for c in range(tq // LANES):
                    cs = slice(c * LANES, (c + 1) * LANES)
                    sc = s_ref[rs, cs]
                    if slot is not None:
                        sc = sc + bd_ref[head, slot, qs, cs]
                    pieces.append(sc)
                mx = pieces[0]
                for sc in pieces[1:]:
                    mx = jnp.maximum(mx, sc)
                m_old = m_ref[rs, :]
                m_new = jnp.maximum(m_old, jnp.max(mx, axis=-1, keepdims=True))
                l_ref[rs, :] = jnp.exp2(m_old - m_new)
                m_ref[rs, :] = m_new
                for c, sc in enumerate(pieces):
                    p_ref[rs, c * LANES:(c + 1) * LANES] = jnp.exp2(sc - m_new).astype(BF16)
            pv = _dot(p_ref[hs, :], v_refs[hv][0, 0, start:start + tq, :])
            a_ref[hs, :] = l_ref[hs, :] * a_ref[hs, :] + pv

    def step(qt):
        q0 = qt * tq
        for m_ref, a_ref in ((sm_ref, sa_ref), (wm_ref, wa_ref)):
            m_ref[...] = jnp.full(m_ref.shape, M_INIT, F32)
            a_ref[...] = jnp.zeros(a_ref.shape, F32)

        k_both = jnp.concatenate([kc_ref[0, 0], kw_ref[0, 0, q0:q0 + tq, :]], axis=0)
        for hv, vc_ref in enumerate((vce_ref, vco_ref)):
            both = _dot_nt(q_rows(q0, hv), k_both)
            cs_ref[halves[hv], :] = both[:, :n_cmp]
            win_sets[0][0][halves[hv], :] = both[:, n_cmp:]
            for rs, head, qs in row_blocks(hv):
                s = cs_ref[rs, :] + bc_ref[qt, head, qs, :]
                m = jnp.maximum(jnp.max(s, axis=-1, keepdims=True), M_INIT)
                e = jnp.exp2(s - m)
                l = jnp.sum(e, axis=-1, keepdims=True)
                p = e / jnp.where(l > 0.0, l, 1.0)
                pc_ref[rs, :] = p.astype(BF16)
                if head == 0:
                    ps_ref[qs, :] = p
                else:
                    ps_ref[qs, :] += p
            oc_ref[halves[hv], :] = _dot(pc_ref[halves[hv], :], vc_ref[0, 0])

        for d in range(min(qt, WINDOW // tq) + 1):
            tile(q0, *win_kv, (qt - d) * tq, d, win_sets[d % 2], wm_ref, wa_ref, have_scores=d == 0)

        psum = ps_ref[...]
        p_hi = psum.astype(BF16)
        r1 = psum - p_hi.astype(F32)
        p_mid = r1.astype(BF16)
        p_lo = (r1 - p_mid.astype(F32)).astype(BF16)
        n_blk = min(n_sel, -(-(q0 + tq) // (SEL_LEN * RANK_ROWS)) * RANK_ROWS)
        ov = ov_ref[:n_blk, :]
        imp = _dot_nt(ov, p_hi) + _dot_nt(ov, p_mid) + _dot_nt(ov, p_lo)
        blk = lax.broadcasted_iota(jnp.int32, (n_blk, tq), 0)
        cur = (q0 + lax.broadcasted_iota(jnp.int32, (n_blk, tq), 1)) // SEL_LEN
        valid = blk <= cur
        forced = jnp.logical_and(valid, jnp.logical_or(blk == 0, blk > cur - N_LOCAL_BLOCKS))
        val = jnp.where(forced, -NEG, jnp.where(valid, imp, NEG))
        rank = jnp.zeros((n_blk, tq), F32)
        for j in range(n_blk):
            rj = val[j:j + 1, :]
            ahead = jnp.logical_or(rj > val, jnp.logical_and(rj == val, blk > j))
            rank = rank + jnp.where(ahead, 1.0, 0.0)
        feat = jnp.where(rank < float(min(SEL_TOP_N, n_sel)), 0.0, NEG)
        feat = jnp.concatenate([jnp.zeros((SEL_F0, tq), F32), feat,
                                jnp.zeros((LANES - SEL_F0 - n_blk, tq), F32)], axis=0)
        feat_t = feat.T
        for slot in range(hg):
            qa_ref[slot * tq:(slot + 1) * tq, :] = (q_ref[0, slot, q0:q0 + tq, :].astype(F32) + feat_t).astype(BF16)

        for i, p in enumerate(range(qt, -1, -1)):
            back = qt - p
            tile(qa_ref, *sel_kv, p * tq, back if back < 2 else None, sel_sets[i % 2], sm_ref, sa_ref)

        gates = gate_ref[0, 0, q0:q0 + tq, :]
        g_hi = gates.astype(BF16)
        g_lo = (gates - g_hi.astype(F32)).astype(BF16)
        first = lax.broadcasted_iota(jnp.int32, gates.shape, 1) < GATE_COLS
        gx = _dot(jnp.where(first, g_hi, g_lo), gx_ref[...])
        low = lax.broadcasted_iota(jnp.int32, (tq, LANES), 1) < HEAD_DIM
        n_pairs = hg // 2
        for j in range(n_pairs):
            even = slice(j * tq, (j + 1) * tq)
            odd = slice(half_rows + j * tq, half_rows + (j + 1) * tq)
            out = gx[:, j * LANES:(j + 1) * LANES] * (oc_ref[even, :] + oc_ref[odd, :])
            for br, a_ref in ((1, sa_ref), (2, wa_ref)):
                a_e = a_ref[even, :]
                a_o = a_ref[odd, :]
                num = jnp.where(low, a_e, a_o)
                den = pltpu.roll(jnp.where(low, a_o, a_e), HEAD_DIM, 1)
                col = (br * n_pairs + j) * LANES
                out = out + gx[:, col:col + LANES] * (num / jnp.where(den > 0.0, den, 1.0))
            o_ref[0, q0:q0 + tq, j * LANES:(j + 1) * LANES] = out.astype(BF16)

    for c in range(n_qt):
        step(c)


def _attention(q, kvc, kv6, bias_c, bias_d, gates, ov_t, gate_expand):
    bsz, _, seq, _ = q.shape
    tq = min(ATT_Q, seq)
    hg = HEADS_PER_GROUP
    rows = hg * tq
    n_cmp = kvc.shape[3]
    assert ov_t.shape[0] <= MAX_SEL_BLOCKS and n_cmp <= tq and tq % LANES == 0
    kv_spec = lambda slot: pl.BlockSpec(
        (1, 1, seq, LANES), lambda b, g, slot=slot: (slot * bsz + b, g, 0, 0))
    kv_flat = kv6.reshape(KV_SLOTS * bsz, N_KV_GROUPS, seq, LANES)
    cmp_spec = lambda slot: pl.BlockSpec(
        (1, 1, n_cmp, LANES), lambda b, g, slot=slot: (slot * bsz + b, g, 0, 0))
    cmp_flat = kvc.reshape(3 * bsz, N_KV_GROUPS, n_cmp, LANES)
    stats = [pltpu.VMEM((rows, LANES), F32)] * 4
    tile_set = [pltpu.VMEM((rows, tq), F32), pltpu.VMEM((rows, tq), BF16), pltpu.VMEM((rows, LANES), F32)]
    return pl.pallas_call(
        _attn_body,
        grid=(bsz, N_KV_GROUPS),
        in_specs=[
            pl.BlockSpec((1, hg, seq, LANES), lambda b, g: (b, g, 0, 0)),
            cmp_spec(0), cmp_spec(1), cmp_spec(2),
            kv_spec(0), kv_spec(1), kv_spec(2), kv_spec(3), kv_spec(4), kv_spec(5),
            pl.BlockSpec((seq // tq, hg, tq, n_cmp), lambda b, g: (0, g, 0, 0)),
            pl.BlockSpec((hg, 3, tq, tq), lambda b, g: (g, 0, 0, 0)),
            pl.BlockSpec((1, 1, seq, gates.shape[-1]), lambda b, g: (b, g, 0, 0)),
            _const_spec(ov_t.shape), _const_spec(gate_expand.shape),
        ],
        out_specs=pl.BlockSpec((1, seq, hg * HEAD_DIM), lambda b, g: (b, 0, g)),
        out_shape=jax.ShapeDtypeStruct((bsz, seq, N_HEADS * HEAD_DIM), BF16),
        scratch_shapes=[
            pltpu.VMEM((tq, n_cmp), F32), pltpu.VMEM((rows, n_cmp), BF16), pltpu.VMEM((rows, n_cmp), F32),
            pltpu.VMEM((rows, LANES), F32), pltpu.VMEM((rows, LANES), BF16),
        ] + stats + tile_set * 4,
        compiler_params=_cparams(2),
        name="nsa_attn",
    )(q, cmp_flat, cmp_flat, cmp_flat, *([kv_flat] * KV_SLOTS), bias_c, bias_d, gates, ov_t, gate_expand)


def _oproj_body(h_ref, a_ref, w_ref, g_ref, b_ref, o_ref):
    mix = _dot(a_ref[...], w_ref[...])
    o_ref[...] = _layer_norm(DN_ALPHA * h_ref[...] + mix, g_ref[...], b_ref[...])


def _oproj(h, a, w_o, g, b):
    t, d = h.shape
    tm = min(PROJ_ROWS, t)
    return pl.pallas_call(
        _oproj_body,
        grid=(t // tm,),
        in_specs=[
            pl.BlockSpec((tm, d), lambda i: (i, 0)),
            pl.BlockSpec((tm, a.shape[1]), lambda i: (i, 0)),
            _const_spec(w_o.shape), _const_spec((1, d)), _const_spec((1, d)),
        ],
        out_specs=pl.BlockSpec((tm, d), lambda i: (i, 0)),
        out_shape=jax.ShapeDtypeStruct((t, d), F32),
        compiler_params=_cparams(1),
        name="oproj",
    )(h, a, w_o, g, b)


def _rel_bucket_np(dist):
    n = np.maximum(dist, 0)
    max_exact = REL_BUCKETS // 2
    nf = np.maximum(n, 1).astype(np.float32)
    large = max_exact + (np.log(nf / np.float32(max_exact)) / np.float32(math.log(REL_MAX_DIST / max_exact))
                         * np.float32(REL_BUCKETS - max_exact)).astype(np.int32)
    return np.where(n < max_exact, n, np.minimum(large, REL_BUCKETS - 1)).astype(np.int32)


def _bias_tables(rel_table, seq, tq, n_cmp_pad):
    tbl2 = rel_table * LOG2E
    far = REL_BUCKETS - 1
    c_hi = tbl2[far].astype(BF16)
    c_lo = (tbl2[far] - c_hi.astype(F32)).astype(BF16)
    c2 = c_hi.astype(F32) + c_lo.astype(F32)

    def lookup(buckets, visible, head_axis, offset):
        shape = [1] * buckets.ndim
        shape[head_axis] = N_HEADS
        b = jnp.asarray(np.where(visible, buckets, -1).astype(np.int8))
        out = jnp.full(shape, NEG, F32)
        for bucket in range(REL_BUCKETS):
            out = jnp.where(b == bucket, (tbl2[bucket] - offset).reshape(shape), out)
        return out

    q = np.arange(tq)[:, None]
    k = np.arange(tq)[None, :]
    d0 = q - k
    d1 = q - k + tq
    d2 = q - k + 2 * tq
    assert tq >= REL_MAX_DIST and 2 * tq == WINDOW
    buckets = np.stack([_rel_bucket_np(d0), _rel_bucket_np(d1), _rel_bucket_np(d2)])[None]
    visible = np.stack([d0 >= 0, np.ones((tq, tq), bool), d2 < WINDOW])[None]
    bias_d = lookup(buckets, visible, 0, c2)

    n_qt = seq // tq
    t = (np.arange(n_qt)[:, None, None, None] * tq + np.arange(tq)[None, None, :, None])
    i = np.arange(n_cmp_pad)[None, None, None, :]
    dc = t - (i * CMP_STRIDE + CMP_LEN - 1)
    n_cmp = seq // CMP_STRIDE - CMP_LEN // CMP_STRIDE + 1
    vis_c = np.logical_and(dc >= 0, i < n_cmp)
    bias_c = lookup(_rel_bucket_np(dc), vis_c, 1, 0.0)

    cfeat = jnp.zeros((N_HEADS, 1, LANES - HEAD_DIM), BF16)
    cfeat = cfeat.at[:, 0, BIAS_F0 - HEAD_DIM].set(c_hi).at[:, 0, BIAS_F0 - HEAD_DIM + 1].set(c_lo)
    return bias_c, bias_d, cfeat


def _overlap_t(n_cmp_pad, n_sel):
    n_cmp = n_cmp_pad - CMP_LEN // CMP_STRIDE + 1
    i = np.arange(n_cmp_pad)[None, :]
    j = np.arange(n_sel)[:, None]
    ov = (i * CMP_STRIDE < (j + 1) * SEL_LEN) & (i * CMP_STRIDE + CMP_LEN > j * SEL_LEN) & (i < n_cmp)
    return ov.astype(np.float32)


def _gate_expand():
    n_pairs = HEADS_PER_GROUP // 2
    ex = np.zeros((GATE_COLS, N_BRANCH * n_pairs * LANES), np.float32)
    for head in range(HEADS_PER_GROUP):
        pair, odd = divmod(head, 2)
        for br in range(N_BRANCH):
            col = (br * n_pairs + pair) * LANES + odd * HEAD_DIM
            ex[N_BRANCH * head + br, col:col + HEAD_DIM] = 1.0
    return np.concatenate([ex, ex], axis=0)


def kernel(x, rel_table, ln_g, ln_b, ffn_w1, ffn_w3, ffn_w2, sgu_w_in, sgu_ln_g, sgu_ln_b, sgu_w_s,
           sgu_b_s, sgu_w_out, kv_w, cmp_pe, cmp_w1, cmp_b1, cmp_w2, nsa_w_qg, nsa_w_o):
    bsz, seq, d = x.shape
    t = bsz * seq
    h = x.reshape(t, d)
    n_chunks = seq // CMP_STRIDE
    tq = min(ATT_Q, seq)

    def ffn(h, layer, half, norm):
        return _ffn(h, ffn_w1[layer, half].astype(BF16), ffn_w3[layer, half].astype(BF16),
                    ffn_w2[layer, half].astype(BF16), ln_g[layer, norm][None], ln_b[layer, norm][None])

    shared = None
    for layer in range(DEPTH):
        if layer == N_A_LAYERS:
            kv_cmp, kv6 = _kv_proj(h.reshape(bsz, seq, d), kv_w.astype(BF16))
            r = kv_cmp.reshape(bsz, n_chunks, CMP_STRIDE, 2, N_KV_GROUPS, HEAD_DIM)
            r = r.transpose(3, 0, 4, 1, 2, 5).reshape(2, bsz, N_KV_GROUPS, n_chunks, CMP_STRIDE * HEAD_DIM)
            zpad = jnp.zeros(cmp_w2.shape[1:], cmp_w2.dtype)
            w2_pad = jnp.stack([jnp.concatenate([cmp_w2[0], zpad], axis=1),
                                jnp.concatenate([cmp_w2[1], zpad], axis=1),
                                jnp.concatenate([zpad, cmp_w2[1]], axis=1)]).astype(BF16)
            kvc = _compress(r, cmp_pe.reshape(2, 1, CMP_LEN * HEAD_DIM), cmp_w1.astype(BF16),
                            cmp_b1[:, None, :], w2_pad)
            bias_c, bias_d, cfeat = _bias_tables(rel_table, seq, tq, n_chunks)
            ov_t = jnp.asarray(_overlap_t(n_chunks, seq // SEL_LEN), BF16)
            shared = (kvc, kv6, bias_c, bias_d, cfeat, ov_t, jnp.asarray(_gate_expand(), BF16))
        h = ffn(h, layer, 0, 0)
        if layer < N_A_LAYERS:
            a = layer
            h = _sgu(h, sgu_w_in[a].astype(BF16), sgu_ln_g[a][None], sgu_ln_b[a][None], sgu_w_s[a],
                     sgu_b_s[a].T, sgu_w_out[a].astype(BF16), ln_g[layer, 1][None], ln_b[layer, 1][None])
        else:
            bl = layer - N_A_LAYERS
            kvc, kv6, bias_c, bias_d, cfeat, ov_t, gate_expand = shared
            nq = N_HEADS * HEAD_DIM
            wq = nsa_w_qg[bl][:, :nq].astype(BF16)
            wg = nsa_w_qg[bl][:, nq:].reshape(d, N_KV_GROUPS, 3 * HEADS_PER_GROUP)
            wg = jnp.pad(wg, ((0, 0), (0, 0), (0, GATE_COLS - 3 * HEADS_PER_GROUP)))
            wg = wg.reshape(d, N_KV_GROUPS * GATE_COLS).astype(BF16)
            q, gates = _qg_proj(h.reshape(bsz, seq, d), wq, wg, cfeat)
            att = _attention(q, kvc, kv6, bias_c, bias_d, gates, ov_t, gate_expand)
            h = _oproj(h, att.reshape(t, nq), nsa_w_o[bl].astype(BF16), ln_g[layer, 1][None], ln_b[layer, 1][None])
        h = ffn(h, layer, 1, 2)
    return h.reshape(bsz, seq, d)
```

```python
import functools
import math

import numpy as np
import jax
import jax.numpy as jnp
from jax import lax
from jax.experimental import pallas as pl
from jax.experimental.pallas import tpu as pltpu

DEPTH = 4
N_A_LAYERS = DEPTH // 2
DN_ALPHA = (2.0 * DEPTH) ** 0.25
LN_EPS = 1e-5
SGU_GROUPS = 8
SGU_CHUNK = 128
N_HEADS = 16
N_KV_GROUPS = 4
HEADS_PER_GROUP = N_HEADS // N_KV_GROUPS
HEAD_DIM = 64
CMP_LEN = 32
CMP_STRIDE = 16
SEL_LEN = 64
SEL_TOP_N = 8
N_LOCAL_BLOCKS = 2
WINDOW = 512
REL_BUCKETS = 32
REL_MAX_DIST = 128
NEG = -1e30
M_INIT = -1e20
LOG2E = 1.0 / math.log(2.0)

BF16 = jnp.bfloat16
F32 = jnp.float32

FFN_ROWS = 1024
FFN_SUB = 512
FFN_COLS = 256
SGU_ROWS = 512
PROJ_ROWS = 512
ATT_Q = 256
ATT_ROWS = 128
RANK_ROWS = 16
LANES = 128
SEL_F0 = HEAD_DIM
BIAS_F0 = 96
MAX_SEL_BLOCKS = BIAS_F0 - SEL_F0
GATE_COLS = 32
HEAD_ORDER = (0, 2, 1, 3)
N_BRANCH = 3
KV_SLOTS = 6
VMEM_LIMIT = 56 * 1024 * 1024


def _cparams(n_axes):
    return pltpu.CompilerParams(
        dimension_semantics=("arbitrary",) * n_axes, vmem_limit_bytes=VMEM_LIMIT)


def _const_spec(shape):
    nd = len(shape)
    return pl.BlockSpec(shape, lambda *_: (0,) * nd)


def _layer_norm(y, g, b):
    mu = jnp.mean(y, axis=-1, keepdims=True)
    yc = y - mu
    var = jnp.mean(yc * yc, axis=-1, keepdims=True)
    return yc * lax.rsqrt(var + LN_EPS) * g + b


def _gelu(x):
    return 0.5 * x * (1.0 + lax.erf(x * math.sqrt(0.5)))


def _dot(a, b):
    return jnp.dot(a, b, preferred_element_type=F32)


def _dot_nt(a, b):
    return lax.dot_general(a, b, (((1,), (1,)), ((), ())), preferred_element_type=F32)


def _ffn_body(*refs, with_mixer):
    if with_mixer:
        h_ref, att_ref, wo_ref, gm_ref, bm_ref, w1_ref, w3_ref, w2_ref, g_ref, b_ref, o_ref, hid_ref = refs
    else:
        h_ref, w1_ref, w3_ref, w2_ref, g_ref, b_ref, o_ref, hid_ref = refs
    ffn = w1_ref.shape[1]
    for r in range(h_ref.shape[0] // FFN_SUB):
        rs = slice(r * FFN_SUB, (r + 1) * FFN_SUB)
        x = h_ref[rs, :]
        if with_mixer:
            x = _layer_norm(DN_ALPHA * x + _dot(att_ref[rs, :], wo_ref[...]), gm_ref[...], bm_ref[...])
        xb = x.astype(BF16)
        for c in range(ffn // FFN_COLS):
            cs = slice(c * FFN_COLS, (c + 1) * FFN_COLS)
            a = _dot(xb, w1_ref[:, cs])
            b3 = _dot(xb, w3_ref[:, cs])
            hid_ref[rs, cs] = (jax.nn.silu(a) * b3).astype(BF16)
        f = _dot(hid_ref[rs, :], w2_ref[...])
        o_ref[rs, :] = _layer_norm(DN_ALPHA * x + 0.5 * f, g_ref[...], b_ref[...])


def _ffn(h, w1, w3, w2, g, b, mixer=None):
    t, d = h.shape
    ffn = w1.shape[1]
    tm = min(FFN_ROWS, t)
    assert tm % FFN_SUB == 0
    rows = lambda width: pl.BlockSpec((tm, width), lambda i: (i, 0))
    mixer_args, mixer_specs = (), []
    if mixer is not None:
        att, w_o, gm, bm = mixer
        mixer_args = (att, w_o, gm, bm)
        mixer_specs = [rows(att.shape[1]), _const_spec(w_o.shape), _const_spec((1, d)), _const_spec((1, d))]
    return pl.pallas_call(
        functools.partial(_ffn_body, with_mixer=mixer is not None),
        grid=(t // tm,),
        in_specs=[rows(d)] + mixer_specs + [
            _const_spec((d, ffn)), _const_spec((d, ffn)), _const_spec((ffn, d)),
            _const_spec((1, d)), _const_spec((1, d)),
        ],
        out_specs=rows(d),
        out_shape=jax.ShapeDtypeStruct((t, d), F32),
        scratch_shapes=[pltpu.VMEM((tm, ffn), BF16)],
        compiler_params=_cparams(1),
        name="ffn_mix" if mixer is not None else "ffn",
    )(h, *mixer_args, w1, w3, w2, g, b)


def _sgu_body(h_ref, win_ref, lg_ref, lb_ref, ws_ref, bs_ref, wout_ref, g_ref, b_ref,
              o_ref, v_ref, gat_ref):
    x = h_ref[...]
    xb = x.astype(BF16)
    tm = x.shape[0]
    half = wout_ref.shape[0]
    gd = half // SGU_GROUPS
    for c in range(SGU_GROUPS):
        cs = slice(c * gd, (c + 1) * gd)
        v_ref[:, cs] = _gelu(_dot(xb, win_ref[:, half + c * gd: half + (c + 1) * gd]))
    v = v_ref[...]
    mu = jnp.mean(v, axis=-1, keepdims=True)
    var = jnp.mean((v - mu) * (v - mu), axis=-1, keepdims=True)
    rstd = lax.rsqrt(var + LN_EPS)
    row = lax.broadcasted_iota(jnp.int32, (SGU_CHUNK, SGU_CHUNK), 0)
    col = lax.broadcasted_iota(jnp.int32, (SGU_CHUNK, SGU_CHUNK), 1)
    causal = col <= row
    for c in range(SGU_GROUPS):
        cs = slice(c * gd, (c + 1) * gd)
        vn = ((v_ref[:, cs] - mu) * rstd * lg_ref[:, cs] + lb_ref[:, cs]).astype(BF16)
        u = _gelu(_dot(xb, win_ref[:, cs]))
        w = jnp.where(causal, ws_ref[c], 0.0).astype(BF16)
        bias = bs_ref[:, c:c + 1]
        for r in range(tm // SGU_CHUNK):
            rs = slice(r * SGU_CHUNK, (r + 1) * SGU_CHUNK)
            mixed = _dot(w, vn[rs]) + bias
            gat_ref[rs, cs] = (u[rs] * mixed).astype(BF16)
    mix = _dot(gat_ref[...], wout_ref[...])
    o_ref[...] = _layer_norm(DN_ALPHA * x + mix, g_ref[...], b_ref[...])


def _sgu(h, w_in, ln_g, ln_b, w_s, b_s_t, w_out, g, b):
    t, d = h.shape
    hidden = w_in.shape[1]
    half = hidden // 2
    tm = min(SGU_ROWS, t)
    return pl.pallas_call(
        _sgu_body,
        grid=(t // tm,),
        in_specs=[
            pl.BlockSpec((tm, d), lambda i: (i, 0)),
            _const_spec((d, hidden)), _const_spec((1, half)), _const_spec((1, half)),
            _const_spec(w_s.shape), _const_spec(b_s_t.shape), _const_spec((half, d)),
            _const_spec((1, d)), _const_spec((1, d)),
        ],
        out_specs=pl.BlockSpec((tm, d), lambda i: (i, 0)),
        out_shape=jax.ShapeDtypeStruct((t, d), F32),
        scratch_shapes=[pltpu.VMEM((tm, half), F32), pltpu.VMEM((tm, half), BF16)],
        compiler_params=_cparams(1),
        name="sgu",
    )(h, w_in, ln_g, ln_b, w_s, b_s_t, w_out, g, b)


def _kv_body(h_ref, w_ref, cmp_ref, kv_ref):
    xb = h_ref[0].astype(BF16)
    ts = xb.shape[0]
    n_cmp_cols = cmp_ref.shape[-1]
    cmp_ref[0] = _dot(xb, w_ref[:, :n_cmp_cols]).astype(cmp_ref.dtype)
    gw = N_KV_GROUPS * HEAD_DIM
    pad = LANES - HEAD_DIM
    pos = pl.program_id(1) * ts + lax.broadcasted_iota(jnp.int32, (ts, pad), 0)
    lane = lax.broadcasted_iota(jnp.int32, (ts, pad), 1) + HEAD_DIM
    bias_lanes = jnp.logical_and(lane >= BIAS_F0, lane < BIAS_F0 + 2)
    blk_lane = jnp.logical_and(lane < BIAS_F0, pos // SEL_LEN == lane - SEL_F0)
    one = lambda m: jnp.where(m, 1.0, 0.0).astype(BF16)
    k_ext = (one(jnp.logical_or(bias_lanes, blk_lane)), one(bias_lanes))
    ones = jnp.ones((ts, pad), BF16)
    for s in range(4):
        y = _dot(xb, w_ref[:, n_cmp_cols + s * gw: n_cmp_cols + (s + 1) * gw]).astype(BF16)
        out = 3 * (s // 2)
        for g in range(N_KV_GROUPS):
            piece = y[:, g * HEAD_DIM:(g + 1) * HEAD_DIM]
            if s % 2 == 0:
                kv_ref[out, 0, g, :, :HEAD_DIM] = piece
                kv_ref[out, 0, g, :, HEAD_DIM:] = k_ext[s // 2]
            else:
                kv_ref[out + 1, 0, g, :, :HEAD_DIM] = piece
                kv_ref[out + 1, 0, g, :, HEAD_DIM:] = ones
                kv_ref[out + 2, 0, g, :, :HEAD_DIM] = ones
                kv_ref[out + 2, 0, g, :, HEAD_DIM:] = piece


def _kv_proj(h3, kv_w):
    bsz, seq, d = h3.shape
    gw = N_KV_GROUPS * HEAD_DIM
    ts = min(PROJ_ROWS, seq)
    return pl.pallas_call(
        _kv_body,
        grid=(bsz, seq // ts),
        in_specs=[pl.BlockSpec((1, ts, d), lambda b, i: (b, i, 0)), _const_spec(kv_w.shape)],
        out_specs=[
            pl.BlockSpec((1, ts, 2 * gw), lambda b, i: (b, i, 0)),
            pl.BlockSpec((KV_SLOTS, 1, N_KV_GROUPS, ts, LANES), lambda b, i: (0, b, 0, i, 0)),
        ],
        out_shape=[
            jax.ShapeDtypeStruct((bsz, seq, 2 * gw), BF16),
            jax.ShapeDtypeStruct((KV_SLOTS, bsz, N_KV_GROUPS, seq, LANES), BF16),
        ],
        compiler_params=_cparams(2),
        name="kv_proj",
    )(h3, kv_w)


def _cmp_body(r_ref, pe_ref, w1_ref, b1_ref, w2_ref, o_ref):
    n_chunks = r_ref.shape[3]
    rows = N_KV_GROUPS * n_chunks
    half = r_ref.shape[4]
    r = r_ref[0, 0].reshape(rows, half).astype(F32)
    top = (r + pe_ref[0, :, :half]).astype(BF16)
    bot = (r + pe_ref[0, :, half:]).astype(BF16)
    a = _dot(top, w1_ref[0, :half])
    bm = _dot(bot, w1_ref[0, half:])
    pre = a + pltpu.roll(bm, rows - 1, 0) + b1_ref[0]
    hid = _gelu(pre).astype(BF16)
    o_ref[0, 0] = _dot(hid, w2_ref[0]).astype(BF16).reshape(N_KV_GROUPS, n_chunks, LANES)


def _compress(r, pe, w1, b1, w2):
    _, bsz, g, n_chunks, width = r.shape
    phi = w1.shape[-1]
    n_out = w2.shape[0]
    src = lambda s: jnp.minimum(s, 1)
    return pl.pallas_call(
        _cmp_body,
        grid=(n_out, bsz),
        in_specs=[
            pl.BlockSpec((1, 1, g, n_chunks, width), lambda s, b: (src(s), b, 0, 0, 0)),
            pl.BlockSpec((1, 1, 2 * width), lambda s, b: (src(s), 0, 0)),
            pl.BlockSpec((1, 2 * width, phi), lambda s, b: (src(s), 0, 0)),
            pl.BlockSpec((1, 1, phi), lambda s, b: (src(s), 0, 0)),
            pl.BlockSpec((1, phi, LANES), lambda s, b: (s, 0, 0)),
        ],
        out_specs=pl.BlockSpec((1, 1, g, n_chunks, LANES), lambda s, b: (s, b, 0, 0, 0)),
        out_shape=jax.ShapeDtypeStruct((n_out, bsz, g, n_chunks, LANES), BF16),
        compiler_params=_cparams(2),
        name="compress",
    )(r, pe, w1, b1, w2)


def _qg_body(h_ref, wq_ref, wg_ref, cf_ref, q_ref, gate_ref):
    xb = h_ref[0].astype(BF16)
    ts = xb.shape[0]
    q = (_dot(xb, wq_ref[...]) * (HEAD_DIM ** -0.5 * LOG2E)).astype(BF16)
    for slot in range(N_HEADS):
        hd = slot - slot % HEADS_PER_GROUP + HEAD_ORDER[slot % HEADS_PER_GROUP]
        q_ref[0, slot, :, :HEAD_DIM] = q[:, hd * HEAD_DIM:(hd + 1) * HEAD_DIM]
        q_ref[0, slot, :, HEAD_DIM:] = jnp.broadcast_to(cf_ref[hd], (ts, LANES - HEAD_DIM))
    gate = jax.nn.sigmoid(_dot(xb, wg_ref[...]))
    for g in range(N_KV_GROUPS):
        piece = gate[:, g * GATE_COLS:(g + 1) * GATE_COLS]
        gate_ref[0, g, :, :GATE_COLS] = piece
        gate_ref[0, g, :, GATE_COLS:] = piece


def _qg_proj(h3, wq, wg, cfeat):
    bsz, seq, d = h3.shape
    ts = min(PROJ_ROWS, seq)
    gcols = 2 * GATE_COLS
    return pl.pallas_call(
        _qg_body,
        grid=(bsz, seq // ts),
        in_specs=[pl.BlockSpec((1, ts, d), lambda b, i: (b, i, 0)),
                  _const_spec(wq.shape), _const_spec(wg.shape), _const_spec(cfeat.shape)],
        out_specs=[
            pl.BlockSpec((1, N_HEADS, ts, LANES), lambda b, i: (b, 0, i, 0)),
            pl.BlockSpec((1, N_KV_GROUPS, ts, gcols), lambda b, i: (b, 0, i, 0)),
        ],
        out_shape=[
            jax.ShapeDtypeStruct((bsz, N_HEADS, seq, LANES), BF16),
            jax.ShapeDtypeStruct((bsz, N_KV_GROUPS, seq, gcols), F32),
        ],
        compiler_params=_cparams(2),
        name="qg_proj",
    )(h3, wq, wg, cfeat)


def _attn_body(q_ref, kc_ref, vce_ref, vco_ref, ks_ref, vse_ref, vso_ref, kw_ref, vwe_ref, vwo_ref,
               bc_ref, bd_ref, gate_ref, ov_ref, gx_ref, o_ref,
               ps_ref, pc_ref, cs_ref, oc_ref, qa_ref, sm_ref, sa_ref, wm_ref, wa_ref, *tile_scratch):
    hg, tq = q_ref.shape[1], bc_ref.shape[2]
    rows = hg * tq
    half_rows = rows // 2
    n_cmp = kc_ref.shape[2]
    n_sel = ov_ref.shape[0]
    n_qt = q_ref.shape[2] // tq
    rb_per_head = tq // ATT_ROWS
    halves = (slice(0, half_rows), slice(half_rows, rows))
    sets = [tile_scratch[3 * i:3 * i + 3] for i in range(len(tile_scratch) // 3)]
    sel_sets, win_sets = sets[:2], sets[2:]
    sel_kv = (ks_ref, (vse_ref, vso_ref))
    win_kv = (kw_ref, (vwe_ref, vwo_ref))

    def q_rows(src, hv):
        if isinstance(src, int):
            return q_ref[0, hv * (hg // 2):(hv + 1) * (hg // 2), src:src + tq, :].reshape(half_rows, LANES)
        return src[halves[hv], :]

    def row_blocks(hv):
        out = []
        for rb in range(hv * half_rows // ATT_ROWS, (hv + 1) * half_rows // ATT_ROWS):
            slot, part = divmod(rb, rb_per_head)
            out.append((slice(rb * ATT_ROWS, (rb + 1) * ATT_ROWS), HEAD_ORDER[slot],
                        slice(part * ATT_ROWS, (part + 1) * ATT_ROWS)))
        return out

    def tile(src_ref, k_ref, v_refs, start, slot, scratch, m_ref, a_ref, have_scores=False):
        s_ref, p_ref, l_ref = scratch
        k = k_ref[0, 0, start:start + tq, :]
        for hv in range(2):
            hs = halves[hv]
            if not have_scores:
                s_ref[hs, :] = _dot_nt(q_rows(src_ref, hv), k)
            for rs, head, qs in row_blocks(hv):
                pieces = []
                for c in range(tq // LANES):
                    cs = slice(c * LANES, (c + 1) * LANES)
                    sc = s_ref[rs, cs]
                    if slot is not None:
                        sc = sc + bd_ref[head, slot, qs, cs]
                    pieces.append(sc)
                mx = pieces[0]
                for sc in pieces[1:]:
                    mx = jnp.maximum(mx, sc)
                m_old = m_ref[rs, :]
                m_new = jnp.maximum(m_old, jnp.max(mx, axis=-1, keepdims=True))
                l_ref[rs, :] = jnp.exp2(m_old - m_new)
                m_ref[rs, :] = m_new
                for c, sc in enumerate(pieces):
                    p_ref[rs, c * LANES:(c + 1) * LANES] = jnp.exp2(sc - m_new).astype(BF16)
            pv = _dot(p_ref[hs, :], v_refs[hv][0, 0, start:start + tq, :])
            a_ref[hs, :] = l_ref[hs, :] * a_ref[hs, :] + pv

    def step(qt):
        q0 = qt * tq
        for m_ref, a_ref in ((sm_ref, sa_ref), (wm_ref, wa_ref)):
            m_ref[...] = jnp.full(m_ref.shape, M_INIT, F32)
            a_ref[...] = jnp.zeros(a_ref.shape, F32)

        k_both = jnp.concatenate([kc_ref[0, 0], kw_ref[0, 0, q0:q0 + tq, :]], axis=0)
        for hv, vc_ref in enumerate((vce_ref, vco_ref)):
            both = _dot_nt(q_rows(q0, hv), k_both)
            cs_ref[halves[hv], :] = both[:, :n_cmp]
            win_sets[0][0][halves[hv], :] = both[:, n_cmp:]
            for rs, head, qs in row_blocks(hv):
                s = cs_ref[rs, :] + bc_ref[qt, head, qs, :]
                m = jnp.maximum(jnp.max(s, axis=-1, keepdims=True), M_INIT)
                e = jnp.exp2(s - m)
                l = jnp.sum(e, axis=-1, keepdims=True)
                p = e / jnp.where(l > 0.0, l, 1.0)
                pc_ref[rs, :] = p.astype(BF16)
                if head == 0:
                    ps_ref[qs, :] = p
                else:
                    ps_ref[qs, :] += p
            oc_ref[halves[hv], :] = _dot(pc_ref[halves[hv], :], vc_ref[0, 0])

        for d in range(min(qt, WINDOW // tq) + 1):
            tile(q0, *win_kv, (qt - d) * tq, d, win_sets[d % 2], wm_ref, wa_ref, have_scores=d == 0)

        psum = ps_ref[...]
        p_hi = psum.astype(BF16)
        r1 = psum - p_hi.astype(F32)
        p_mid = r1.astype(BF16)
        p_lo = (r1 - p_mid.astype(F32)).astype(BF16)
        n_blk = min(n_sel, -(-(q0 + tq) // (SEL_LEN * RANK_ROWS)) * RANK_ROWS)
        ov = ov_ref[:n_blk, :]
        imp = _dot_nt(ov, p_hi) + _dot_nt(ov, p_mid) + _dot_nt(ov, p_lo)
        blk = lax.broadcasted_iota(jnp.int32, (n_blk, tq), 0)
        cur = (q0 + lax.broadcasted_iota(jnp.int32, (n_blk, tq), 1)) // SEL_LEN
        valid = blk <= cur
        forced = jnp.logical_and(valid, jnp.logical_or(blk == 0, blk > cur - N_LOCAL_BLOCKS))
        val = jnp.where(forced, -NEG, jnp.where(valid, imp, NEG))
        rank = jnp.zeros((n_blk, tq), F32)
        for j in range(n_blk):
            rj = val[j:j + 1, :]
            ahead = jnp.logical_or(rj > val, jnp.logical_and(rj == val, blk > j))
            rank = rank + jnp.where(ahead, 1.0, 0.0)
        feat = jnp.where(rank < float(min(SEL_TOP_N, n_sel)), 0.0, NEG)
        feat = jnp.concatenate([jnp.zeros((SEL_F0, tq), F32), feat,
                                jnp.zeros((LANES - SEL_F0 - n_blk, tq), F32)], axis=0)
        feat_t = feat.T
        for slot in range(hg):
            qa_ref[slot * tq:(slot + 1) * tq, :] = (q_ref[0, slot, q0:q0 + tq, :].astype(F32) + feat_t).astype(BF16)

        for i, p in enumerate(range(qt, -1, -1)):
            back = qt - p
            tile(qa_ref, *sel_kv, p * tq, back if back < 2 else None, sel_sets[i % 2], sm_ref, sa_ref)

        gates = gate_ref[0, 0, q0:q0 + tq, :]
        g_hi = gates.astype(BF16)
        g_lo = (gates - g_hi.astype(F32)).astype(BF16)
        first = lax.broadcasted_iota(jnp.int32, gates.shape, 1) < GATE_COLS
        gx = _dot(jnp.where(first, g_hi, g_lo), gx_ref[...])
        low = lax.broadcasted_iota(jnp.int32, (tq, LANES), 1) < HEAD_DIM
        n_pairs = hg // 2
        for j in range(n_pairs):
            even = slice(j * tq, (j + 1) * tq)
            odd = slice(half_rows + j * tq, half_rows + (j + 1) * tq)
            out = gx[:, j * LANES:(j + 1) * LANES] * (oc_ref[even, :] + oc_ref[odd, :])
            for br, a_ref in ((1, sa_ref), (2, wa_ref)):
                a_e = a_ref[even, :]
                a_o = a_ref[odd, :]
                num = jnp.where(low, a_e, a_o)
                den = pltpu.roll(jnp.where(low, a_o, a_e), HEAD_DIM, 1)
                col = (br * n_pairs + j) * LANES
                out = out + gx[:, col:col + LANES] * (num / jnp.where(den > 0.0, den, 1.0))
            o_ref[0, q0:q0 + tq, j * LANES:(j + 1) * LANES] = out.astype(BF16)

    for c in range(n_qt):
        step(c)


def _attention(q, kvc, kv6, bias_c, bias_d, gates, ov_t, gate_expand):
    bsz, _, seq, _ = q.shape
    tq = min(ATT_Q, seq)
    hg = HEADS_PER_GROUP
    rows = hg * tq
    n_cmp = kvc.shape[3]
    assert ov_t.shape[0] <= MAX_SEL_BLOCKS and n_cmp <= tq and tq % LANES == 0
    kv_spec = lambda slot: pl.BlockSpec(
        (1, 1, seq, LANES), lambda g, b, slot=slot: (slot * bsz + b, g, 0, 0))
    kv_flat = kv6.reshape(KV_SLOTS * bsz, N_KV_GROUPS, seq, LANES)
    cmp_spec = lambda slot: pl.BlockSpec(
        (1, 1, n_cmp, LANES), lambda g, b, slot=slot: (slot * bsz + b, g, 0, 0))
    cmp_flat = kvc.reshape(3 * bsz, N_KV_GROUPS, n_cmp, LANES)
    stats = [pltpu.VMEM((rows, LANES), F32)] * 4
    tile_set = [pltpu.VMEM((rows, tq), F32), pltpu.VMEM((rows, tq), BF16), pltpu.VMEM((rows, LANES), F32)]
    return pl.pallas_call(
        _attn_body,
        grid=(N_KV_GROUPS, bsz),
        in_specs=[
            pl.BlockSpec((1, hg, seq, LANES), lambda g, b: (b, g, 0, 0)),
            cmp_spec(0), cmp_spec(1), cmp_spec(2),
            kv_spec(0), kv_spec(1), kv_spec(2), kv_spec(3), kv_spec(4), kv_spec(5),
            pl.BlockSpec((seq // tq, hg, tq, n_cmp), lambda g, b: (0, g, 0, 0)),
            pl.BlockSpec((hg, 3, tq, tq), lambda g, b: (g, 0, 0, 0)),
            pl.BlockSpec((1, 1, seq, gates.shape[-1]), lambda g, b: (b, g, 0, 0)),
            _const_spec(ov_t.shape), _const_spec(gate_expand.shape),
        ],
        out_specs=pl.BlockSpec((1, seq, hg * HEAD_DIM), lambda g, b: (b, 0, g)),
        out_shape=jax.ShapeDtypeStruct((bsz, seq, N_HEADS * HEAD_DIM), BF16),
        scratch_shapes=[
            pltpu.VMEM((tq, n_cmp), F32), pltpu.VMEM((rows, n_cmp), BF16), pltpu.VMEM((rows, n_cmp), F32),
            pltpu.VMEM((rows, LANES), F32), pltpu.VMEM((rows, LANES), BF16),
        ] + stats + tile_set * 4,
        compiler_params=_cparams(2),
        name="nsa_attn",
    )(q, cmp_flat, cmp_flat, cmp_flat, *([kv_flat] * KV_SLOTS), bias_c, bias_d, gates, ov_t, gate_expand)


def _rel_bucket_np(dist):
    n = np.maximum(dist, 0)
    max_exact = REL_BUCKETS // 2
    nf = np.maximum(n, 1).astype(np.float32)
    large = max_exact + (np.log(nf / np.float32(max_exact)) / np.float32(math.log(REL_MAX_DIST / max_exact))
                         * np.float32(REL_BUCKETS - max_exact)).astype(np.int32)
    return np.where(n < max_exact, n, np.minimum(large, REL_BUCKETS - 1)).astype(np.int32)


def _bias_tables(rel_table, seq, tq, n_cmp_pad):
    tbl2 = rel_table * LOG2E
    far = REL_BUCKETS - 1
    c_hi = tbl2[far].astype(BF16)
    c_lo = (tbl2[far] - c_hi.astype(F32)).astype(BF16)
    c2 = c_hi.astype(F32) + c_lo.astype(F32)

    def lookup(buckets, visible, head_axis, offset):
        shape = [1] * buckets.ndim
        shape[head_axis] = N_HEADS
        b = jnp.asarray(np.where(visible, buckets, -1).astype(np.int8))
        out = jnp.full(shape, NEG, F32)
        for bucket in range(REL_BUCKETS):
            out = jnp.where(b == bucket, (tbl2[bucket] - offset).reshape(shape), out)
        return out

    q = np.arange(tq)[:, None]
    k = np.arange(tq)[None, :]
    d0 = q - k
    d1 = q - k + tq
    d2 = q - k + 2 * tq
    assert tq >= REL_MAX_DIST and 2 * tq == WINDOW
    buckets = np.stack([_rel_bucket_np(d0), _rel_bucket_np(d1), _rel_bucket_np(d2)])[None]
    visible = np.stack([d0 >= 0, np.ones((tq, tq), bool), d2 < WINDOW])[None]
    bias_d = lookup(buckets, visible, 0, c2)

    n_qt = seq // tq
    t = (np.arange(n_qt)[:, None, None, None] * tq + np.arange(tq)[None, None, :, None])
    i = np.arange(n_cmp_pad)[None, None, None, :]
    dc = t - (i * CMP_STRIDE + CMP_LEN - 1)
    n_cmp = seq // CMP_STRIDE - CMP_LEN // CMP_STRIDE + 1
    vis_c = np.logical_and(dc >= 0, i < n_cmp)
    bias_c = lookup(_rel_bucket_np(dc), vis_c, 1, 0.0)

    cfeat = jnp.zeros((N_HEADS, 1, LANES - HEAD_DIM), BF16)
    cfeat = cfeat.at[:, 0, BIAS_F0 - HEAD_DIM].set(c_hi).at[:, 0, BIAS_F0 - HEAD_DIM + 1].set(c_lo)
    return bias_c, bias_d, cfeat


def _overlap_t(n_cmp_pad, n_sel):
    n_cmp = n_cmp_pad - CMP_LEN // CMP_STRIDE + 1
    i = np.arange(n_cmp_pad)[None, :]
    j = np.arange(n_sel)[:, None]
    ov = (i * CMP_STRIDE < (j + 1) * SEL_LEN) & (i * CMP_STRIDE + CMP_LEN > j * SEL_LEN) & (i < n_cmp)
    return ov.astype(np.float32)


def _gate_expand():
    n_pairs = HEADS_PER_GROUP // 2
    ex = np.zeros((GATE_COLS, N_BRANCH * n_pairs * LANES), np.float32)
    for head in range(HEADS_PER_GROUP):
        pair, odd = divmod(head, 2)
        for br in range(N_BRANCH):
            col = (br * n_pairs + pair) * LANES + odd * HEAD_DIM
            ex[N_BRANCH * head + br, col:col + HEAD_DIM] = 1.0
    return np.concatenate([ex, ex], axis=0)


def kernel(x, rel_table, ln_g, ln_b, ffn_w1, ffn_w3, ffn_w2, sgu_w_in, sgu_ln_g, sgu_ln_b, sgu_w_s,
           sgu_b_s, sgu_w_out, kv_w, cmp_pe, cmp_w1, cmp_b1, cmp_w2, nsa_w_qg, nsa_w_o):
    bsz, seq, d = x.shape
    t = bsz * seq
    h = x.reshape(t, d)
    n_chunks = seq // CMP_STRIDE
    tq = min(ATT_Q, seq)

    def ffn(h, layer, half, norm, mixer=None):
        return _ffn(h, ffn_w1[layer, half].astype(BF16), ffn_w3[layer, half].astype(BF16),
                    ffn_w2[layer, half].astype(BF16), ln_g[layer, norm][None], ln_b[layer, norm][None], mixer)

    shared = None
    for layer in range(DEPTH):
        if layer == N_A_LAYERS:
            kv_cmp, kv6 = _kv_proj(h.reshape(bsz, seq, d), kv_w.astype(BF16))
            r = kv_cmp.reshape(bsz, n_chunks, CMP_STRIDE, 2, N_KV_GROUPS, HEAD_DIM)
            r = r.transpose(3, 0, 4, 1, 2, 5).reshape(2, bsz, N_KV_GROUPS, n_chunks, CMP_STRIDE * HEAD_DIM)
            zpad = jnp.zeros(cmp_w2.shape[1:], cmp_w2.dtype)
            w2_pad = jnp.stack([jnp.concatenate([cmp_w2[0], zpad], axis=1),
                                jnp.concatenate([cmp_w2[1], zpad], axis=1),
                                jnp.concatenate([zpad, cmp_w2[1]], axis=1)]).astype(BF16)
            kvc = _compress(r, cmp_pe.reshape(2, 1, CMP_LEN * HEAD_DIM), cmp_w1.astype(BF16),
                            cmp_b1[:, None, :], w2_pad)
            bias_c, bias_d, cfeat = _bias_tables(rel_table, seq, tq, n_chunks)
            ov_t = jnp.asarray(_overlap_t(n_chunks, seq // SEL_LEN), BF16)
            shared = (kvc, kv6, bias_c, bias_d, cfeat, ov_t, jnp.asarray(_gate_expand(), BF16))
        h = ffn(h, layer, 0, 0)
        mixer = None
        if layer < N_A_LAYERS:
            a = layer
            h = _sgu(h, sgu_w_in[a].astype(BF16), sgu_ln_g[a][None], sgu_ln_b[a][None], sgu_w_s[a],
                     sgu_b_s[a].T, sgu_w_out[a].astype(BF16), ln_g[layer, 1][None], ln_b[layer, 1][None])
        else:
            bl = layer - N_A_LAYERS
            kvc, kv6, bias_c, bias_d, cfeat, ov_t, gate_expand = shared
            nq = N_HEADS * HEAD_DIM
            wq = nsa_w_qg[bl][:, :nq].astype(BF16)
            wg = nsa_w_qg[bl][:, nq:].reshape(d, N_KV_GROUPS, 3 * HEADS_PER_GROUP)
            wg = jnp.pad(wg, ((0, 0), (0, 0), (0, GATE_COLS - 3 * HEADS_PER_GROUP)))
            wg = wg.reshape(d, N_KV_GROUPS * GATE_COLS).astype(BF16)
            q, gates = _qg_proj(h.reshape(bsz, seq, d), wq, wg, cfeat)
            att = _attention(q, kvc, kv6, bias_c, bias_d, gates, ov_t, gate_expand)
            mixer = (att.reshape(t, nq), nsa_w_o[bl].astype(BF16), ln_g[layer, 1][None], ln_b[layer, 1][None])
        h = ffn(h, layer, 1, 2, mixer)
    return h.reshape(bsz, seq, d)
```

```python
import functools
import math

import numpy as np
import jax
import jax.numpy as jnp
from jax import lax
from jax.experimental import pallas as pl
from jax.experimental.pallas import tpu as pltpu

DEPTH = 4
N_A_LAYERS = DEPTH // 2
DN_ALPHA = (2.0 * DEPTH) ** 0.25
LN_EPS = 1e-5
SGU_GROUPS = 8
SGU_CHUNK = 128
N_HEADS = 16
N_KV_GROUPS = 4
HEADS_PER_GROUP = N_HEADS // N_KV_GROUPS
HEAD_DIM = 64
CMP_LEN = 32
CMP_STRIDE = 16
SEL_LEN = 64
SEL_TOP_N = 8
N_LOCAL_BLOCKS = 2
WINDOW = 512
REL_BUCKETS = 32
REL_MAX_DIST = 128
NEG = -1e30
M_INIT = -1e20
LOG2E = 1.0 / math.log(2.0)

BF16 = jnp.bfloat16
F32 = jnp.float32

FFN_ROWS = 1024
FFN_SUB = 512
FFN_COLS = 256
SGU_ROWS = 512
PROJ_ROWS = 512
ATT_Q = 256
ATT_ROWS = 128
RANK_ROWS = 16
LANES = 128
SEL_F0 = HEAD_DIM
BIAS_F0 = 96
MAX_SEL_BLOCKS = BIAS_F0 - SEL_F0
GATE_COLS = 32
HEAD_ORDER = (0, 2, 1, 3)
N_BRANCH = 3
KV_SLOTS = 6
VMEM_LIMIT = 56 * 1024 * 1024


def _cparams(n_axes):
    return pltpu.CompilerParams(
        dimension_semantics=("arbitrary",) * n_axes, vmem_limit_bytes=VMEM_LIMIT)


def _const_spec(shape):
    nd = len(shape)
    return pl.BlockSpec(shape, lambda *_: (0,) * nd)


def _layer_norm(y, g, b):
    mu = jnp.mean(y, axis=-1, keepdims=True)
    yc = y - mu
    var = jnp.mean(yc * yc, axis=-1, keepdims=True)
    return yc * lax.rsqrt(var + LN_EPS) * g + b


def _gelu(x):
    return 0.5 * x * (1.0 + lax.erf(x * math.sqrt(0.5)))


def _dot(a, b):
    return jnp.dot(a, b, preferred_element_type=F32)


def _dot_nt(a, b):
    return lax.dot_general(a, b, (((1,), (1,)), ((), ())), preferred_element_type=F32)


def _ffn_body(*refs, with_mixer):
    if with_mixer:
        h_ref, att_ref, wo_ref, gm_ref, bm_ref, w1_ref, w3_ref, w2_ref, g_ref, b_ref, o_ref, hid_ref = refs
    else:
        h_ref, w1_ref, w3_ref, w2_ref, g_ref, b_ref, o_ref, hid_ref = refs
    ffn = w1_ref.shape[1]
    for r in range(h_ref.shape[0] // FFN_SUB):
        rs = slice(r * FFN_SUB, (r + 1) * FFN_SUB)
        x = h_ref[rs, :]
        if with_mixer:
            x = _layer_norm(DN_ALPHA * x + _dot(att_ref[rs, :], wo_ref[...]), gm_ref[...], bm_ref[...])
        xb = x.astype(BF16)
        for c in range(ffn // FFN_COLS):
            cs = slice(c * FFN_COLS, (c + 1) * FFN_COLS)
            a = _dot(xb, w1_ref[:, cs])
            b3 = _dot(xb, w3_ref[:, cs])
            hid_ref[rs, cs] = (jax.nn.silu(a) * b3).astype(BF16)
        f = _dot(hid_ref[rs, :], w2_ref[...])
        o_ref[rs, :] = _layer_norm(DN_ALPHA * x + 0.5 * f, g_ref[...], b_ref[...])


def _ffn(h, w1, w3, w2, g, b, mixer=None):
    t, d = h.shape
    ffn = w1.shape[1]
    tm = min(FFN_ROWS, t)
    assert tm % FFN_SUB == 0
    rows = lambda width: pl.BlockSpec((tm, width), lambda i: (i, 0))
    mixer_args, mixer_specs = (), []
    if mixer is not None:
        att, w_o, gm, bm = mixer
        mixer_args = (att, w_o, gm, bm)
        mixer_specs = [rows(att.shape[1]), _const_spec(w_o.shape), _const_spec((1, d)), _const_spec((1, d))]
    return pl.pallas_call(
        functools.partial(_ffn_body, with_mixer=mixer is not None),
        grid=(t // tm,),
        in_specs=[rows(d)] + mixer_specs + [
            _const_spec((d, ffn)), _const_spec((d, ffn)), _const_spec((ffn, d)),
            _const_spec((1, d)), _const_spec((1, d)),
        ],
        out_specs=rows(d),
        out_shape=jax.ShapeDtypeStruct((t, d), F32),
        scratch_shapes=[pltpu.VMEM((tm, ffn), BF16)],
        compiler_params=_cparams(1),
        name="ffn_mix" if mixer is not None else "ffn",
    )(h, *mixer_args, w1, w3, w2, g, b)


def _sgu_body(h_ref, win_ref, lg_ref, lb_ref, ws_ref, bs_ref, wout_ref, g_ref, b_ref,
              o_ref, v_ref, gat_ref):
    x = h_ref[...]
    xb = x.astype(BF16)
    tm = x.shape[0]
    half = wout_ref.shape[0]
    gd = half // SGU_GROUPS
    for c in range(SGU_GROUPS):
        cs = slice(c * gd, (c + 1) * gd)
        v_ref[:, cs] = _gelu(_dot(xb, win_ref[:, half + c * gd: half + (c + 1) * gd]))
    v = v_ref[...]
    mu = jnp.mean(v, axis=-1, keepdims=True)
    var = jnp.mean((v - mu) * (v - mu), axis=-1, keepdims=True)
    rstd = lax.rsqrt(var + LN_EPS)
    row = lax.broadcasted_iota(jnp.int32, (SGU_CHUNK, SGU_CHUNK), 0)
    col = lax.broadcasted_iota(jnp.int32, (SGU_CHUNK, SGU_CHUNK), 1)
    causal = col <= row
    for c in range(SGU_GROUPS):
        cs = slice(c * gd, (c + 1) * gd)
        vn = ((v_ref[:, cs] - mu) * rstd * lg_ref[:, cs] + lb_ref[:, cs]).astype(BF16)
        u = _gelu(_dot(xb, win_ref[:, cs]))
        w = jnp.where(causal, ws_ref[c], 0.0).astype(BF16)
        bias = bs_ref[:, c:c + 1]
        for r in range(tm // SGU_CHUNK):
            rs = slice(r * SGU_CHUNK, (r + 1) * SGU_CHUNK)
            mixed = _dot(w, vn[rs]) + bias
            gat_ref[rs, cs] = (u[rs] * mixed).astype(BF16)
    mix = _dot(gat_ref[...], wout_ref[...])
    o_ref[...] = _layer_norm(DN_ALPHA * x + mix, g_ref[...], b_ref[...])


def _sgu(h, w_in, ln_g, ln_b, w_s, b_s_t, w_out, g, b):
    t, d = h.shape
    hidden = w_in.shape[1]
    half = hidden // 2
    tm = min(SGU_ROWS, t)
    return pl.pallas_call(
        _sgu_body,
        grid=(t // tm,),
        in_specs=[
            pl.BlockSpec((tm, d), lambda i: (i, 0)),
            _const_spec((d, hidden)), _const_spec((1, half)), _const_spec((1, half)),
            _const_spec(w_s.shape), _const_spec(b_s_t.shape), _const_spec((half, d)),
            _const_spec((1, d)), _const_spec((1, d)),
        ],
        out_specs=pl.BlockSpec((tm, d), lambda i: (i, 0)),
        out_shape=jax.ShapeDtypeStruct((t, d), F32),
        scratch_shapes=[pltpu.VMEM((tm, half), F32), pltpu.VMEM((tm, half), BF16)],
        compiler_params=_cparams(1),
        name="sgu",
    )(h, w_in, ln_g, ln_b, w_s, b_s_t, w_out, g, b)


def _kv_body(h_ref, w_ref, cmp_ref, kv_ref):
    xb = h_ref[0].astype(BF16)
    ts = xb.shape[0]
    n_cmp_cols = cmp_ref.shape[-1]
    cmp_ref[0] = _dot(xb, w_ref[:, :n_cmp_cols]).astype(cmp_ref.dtype)
    gw = N_KV_GROUPS * HEAD_DIM
    pad = LANES - HEAD_DIM
    pos = pl.program_id(1) * ts + lax.broadcasted_iota(jnp.int32, (ts, pad), 0)
    lane = lax.broadcasted_iota(jnp.int32, (ts, pad), 1) + HEAD_DIM
    bias_lanes = jnp.logical_and(lane >= BIAS_F0, lane < BIAS_F0 + 2)
    blk_lane = jnp.logical_and(lane < BIAS_F0, pos // SEL_LEN == lane - SEL_F0)
    one = lambda m: jnp.where(m, 1.0, 0.0).astype(BF16)
    k_ext = (one(jnp.logical_or(bias_lanes, blk_lane)), one(bias_lanes))
    ones = jnp.ones((ts, pad), BF16)
    for s in range(4):
        y = _dot(xb, w_ref[:, n_cmp_cols + s * gw: n_cmp_cols + (s + 1) * gw]).astype(BF16)
        out = 3 * (s // 2)
        for g in range(N_KV_GROUPS):
            piece = y[:, g * HEAD_DIM:(g + 1) * HEAD_DIM]
            if s % 2 == 0:
                kv_ref[out, 0, g, :, :HEAD_DIM] = piece
                kv_ref[out, 0, g, :, HEAD_DIM:] = k_ext[s // 2]
            else:
                kv_ref[out + 1, 0, g, :, :HEAD_DIM] = piece
                kv_ref[out + 1, 0, g, :, HEAD_DIM:] = ones
                kv_ref[out + 2, 0, g, :, :HEAD_DIM] = ones
                kv_ref[out + 2, 0, g, :, HEAD_DIM:] = piece


def _kv_proj(h3, kv_w):
    bsz, seq, d = h3.shape
    gw = N_KV_GROUPS * HEAD_DIM
    ts = min(PROJ_ROWS, seq)
    return pl.pallas_call(
        _kv_body,
        grid=(bsz, seq // ts),
        in_specs=[pl.BlockSpec((1, ts, d), lambda b, i: (b, i, 0)), _const_spec(kv_w.shape)],
        out_specs=[
            pl.BlockSpec((1, ts, 2 * gw), lambda b, i: (b, i, 0)),
            pl.BlockSpec((KV_SLOTS, 1, N_KV_GROUPS, ts, LANES), lambda b, i: (0, b, 0, i, 0)),
        ],
        out_shape=[
            jax.ShapeDtypeStruct((bsz, seq, 2 * gw), BF16),
            jax.ShapeDtypeStruct((KV_SLOTS, bsz, N_KV_GROUPS, seq, LANES), BF16),
        ],
        compiler_params=_cparams(2),
        name="kv_proj",
    )(h3, kv_w)


def _cmp_body(r_ref, pe_ref, w1_ref, b1_ref, w2_ref, o_ref):
    n_chunks = r_ref.shape[3]
    rows = N_KV_GROUPS * n_chunks
    half = r_ref.shape[4]
    r = r_ref[0, 0].reshape(rows, half).astype(F32)
    top = (r + pe_ref[0, :, :half]).astype(BF16)
    bot = (r + pe_ref[0, :, half:]).astype(BF16)
    a = _dot(top, w1_ref[0, :half])
    bm = _dot(bot, w1_ref[0, half:])
    pre = a + pltpu.roll(bm, rows - 1, 0) + b1_ref[0]
    hid = _gelu(pre).astype(BF16)
    o_ref[0, 0] = _dot(hid, w2_ref[0]).astype(BF16).reshape(N_KV_GROUPS, n_chunks, LANES)


def _compress(r, pe, w1, b1, w2):
    _, bsz, g, n_chunks, width = r.shape
    phi = w1.shape[-1]
    n_out = w2.shape[0]
    src = lambda s: jnp.minimum(s, 1)
    return pl.pallas_call(
        _cmp_body,
        grid=(n_out, bsz),
        in_specs=[
            pl.BlockSpec((1, 1, g, n_chunks, width), lambda s, b: (src(s), b, 0, 0, 0)),
            pl.BlockSpec((1, 1, 2 * width), lambda s, b: (src(s), 0, 0)),
            pl.BlockSpec((1, 2 * width, phi), lambda s, b: (src(s), 0, 0)),
            pl.BlockSpec((1, 1, phi), lambda s, b: (src(s), 0, 0)),
            pl.BlockSpec((1, phi, LANES), lambda s, b: (s, 0, 0)),
        ],
        out_specs=pl.BlockSpec((1, 1, g, n_chunks, LANES), lambda s, b: (s, b, 0, 0, 0)),
        out_shape=jax.ShapeDtypeStruct((n_out, bsz, g, n_chunks, LANES), BF16),
        compiler_params=_cparams(2),
        name="compress",
    )(r, pe, w1, b1, w2)


def _qg_body(h_ref, wq_ref, wg_ref, cf_ref, q_ref, gate_ref):
    xb = h_ref[0].astype(BF16)
    ts = xb.shape[0]
    q = (_dot(xb, wq_ref[...]) * (HEAD_DIM ** -0.5 * LOG2E)).astype(BF16)
    for slot in range(N_HEADS):
        hd = slot - slot % HEADS_PER_GROUP + HEAD_ORDER[slot % HEADS_PER_GROUP]
        q_ref[0, slot, :, :HEAD_DIM] = q[:, hd * HEAD_DIM:(hd + 1) * HEAD_DIM]
        q_ref[0, slot, :, HEAD_DIM:] = jnp.broadcast_to(cf_ref[hd], (ts, LANES - HEAD_DIM))
    gate = jax.nn.sigmoid(_dot(xb, wg_ref[...]))
    for g in range(N_KV_GROUPS):
        piece = gate[:, g * GATE_COLS:(g + 1) * GATE_COLS]
        gate_ref[0, g, :, :GATE_COLS] = piece
        gate_ref[0, g, :, GATE_COLS:] = piece


def _qg_proj(h3, wq, wg, cfeat):
    bsz, seq, d = h3.shape
    ts = min(PROJ_ROWS, seq)
    gcols = 2 * GATE_COLS
    return pl.pallas_call(
        _qg_body,
        grid=(bsz, seq // ts),
        in_specs=[pl.BlockSpec((1, ts, d), lambda b, i: (b, i, 0)),
                  _const_spec(wq.shape), _const_spec(wg.shape), _const_spec(cfeat.shape)],
        out_specs=[
            pl.BlockSpec((1, N_HEADS, ts, LANES), lambda b, i: (b, 0, i, 0)),
            pl.BlockSpec((1, N_KV_GROUPS, ts, gcols), lambda b, i: (b, 0, i, 0)),
        ],
        out_shape=[
            jax.ShapeDtypeStruct((bsz, N_HEADS, seq, LANES), BF16),
            jax.ShapeDtypeStruct((bsz, N_KV_GROUPS, seq, gcols), F32),
        ],
        compiler_params=_cparams(2),
        name="qg_proj",
    )(h3, wq, wg, cfeat)


def _attn_body(q_ref, kc_ref, vce_ref, vco_ref, ks_ref, vse_ref, vso_ref, kw_ref, vwe_ref, vwo_ref,
               bc_ref, bd_ref, gate_ref, ov_ref, gx_ref, o_ref,
               ps_ref, pc_ref, cs_ref, oc_ref, qa_ref, sm_ref, sa_ref, wm_ref, wa_ref, *tile_scratch):
    hg, tq = q_ref.shape[1], bc_ref.shape[2]
    rows = hg * tq
    half_rows = rows // 2
    n_cmp = kc_ref.shape[2]
    n_sel = ov_ref.shape[0]
    n_qt = q_ref.shape[2] // tq
    rb_per_head = tq // ATT_ROWS
    halves = (slice(0, half_rows), slice(half_rows, rows))
    sets = [tile_scratch[3 * i:3 * i + 3] for i in range(len(tile_scratch) // 3)]
    sel_sets, win_sets = sets[:2], sets[2:]
    sel_kv = (ks_ref, (vse_ref, vso_ref))
    win_kv = (kw_ref, (vwe_ref, vwo_ref))

    def q_rows(src, hv):
        if isinstance(src, int):
            return q_ref[0, hv * (hg // 2):(hv + 1) * (hg // 2), src:src + tq, :].reshape(half_rows, LANES)
        return src[halves[hv], :]

    def row_blocks(hv):
        out = []
        for rb in range(hv * half_rows // ATT_ROWS, (hv + 1) * half_rows // ATT_ROWS):
            slot, part = divmod(rb, rb_per_head)
            out.append((slice(rb * ATT_ROWS, (rb + 1) * ATT_ROWS), HEAD_ORDER[slot],
                        slice(part * ATT_ROWS, (part + 1) * ATT_ROWS)))
        return out

    def tile(src_ref, k_ref, v_refs, start, slot, scratch, m_ref, a_ref, have_scores=False, first=False):
        s_ref, p_ref, l_ref = scratch
        k = k_ref[0, 0, start:start + tq, :]
        for hv in range(2):
            hs = halves[hv]
            if not have_scores:
                s_ref[hs, :] = _dot_nt(q_rows(src_ref, hv), k)
            for rs, head, qs in row_blocks(hv):
                pieces = []
                for c in range(tq // LANES):
                    cs = slice(c * LANES, (c + 1) * LANES)
                    sc = s_ref[rs, cs]
                    if slot is not None:
                        sc = sc + bd_ref[head, slot, qs, cs]
                    pieces.append(sc)
                mx = pieces[0]
                for sc in pieces[1:]:
                    mx = jnp.maximum(mx, sc)
                if first:
                    m_new = jnp.broadcast_to(jnp.maximum(jnp.max(mx, axis=-1, keepdims=True), M_INIT),
                                             (ATT_ROWS, LANES))
                else:
                    m_old = m_ref[rs, :]
                    m_new = jnp.maximum(m_old, jnp.max(mx, axis=-1, keepdims=True))
                    l_ref[rs, :] = jnp.exp2(m_old - m_new)
                m_ref[rs, :] = m_new
                for c, sc in enumerate(pieces):
                    p_ref[rs, c * LANES:(c + 1) * LANES] = jnp.exp2(sc - m_new).astype(BF16)
            pv = _dot(p_ref[hs, :], v_refs[hv][0, 0, start:start + tq, :])
            a_ref[hs, :] = pv if first else l_ref[hs, :] * a_ref[hs, :] + pv

    def step(qt):
        q0 = qt * tq

        k_both = jnp.concatenate([kc_ref[0, 0], kw_ref[0, 0, q0:q0 + tq, :]], axis=0)
        for hv, vc_ref in enumerate((vce_ref, vco_ref)):
            both = _dot_nt(q_rows(q0, hv), k_both)
            cs_ref[halves[hv], :] = both[:, :n_cmp]
            win_sets[0][0][halves[hv], :] = both[:, n_cmp:]
            for rs, head, qs in row_blocks(hv):
                s = cs_ref[rs, :] + bc_ref[qt, head, qs, :]
                m = jnp.maximum(jnp.max(s, axis=-1, keepdims=True), M_INIT)
                e = jnp.exp2(s - m)
                l = jnp.sum(e, axis=-1, keepdims=True)
                p = e / jnp.where(l > 0.0, l, 1.0)
                pc_ref[rs, :] = p.astype(BF16)
                if head == 0:
                    ps_ref[qs, :] = p
                else:
                    ps_ref[qs, :] += p
            oc_ref[halves[hv], :] = _dot(pc_ref[halves[hv], :], vc_ref[0, 0])

        for d in range(min(qt, WINDOW // tq) + 1):
            tile(q0, *win_kv, (qt - d) * tq, d, win_sets[d % 2], wm_ref, wa_ref, have_scores=d == 0, first=d == 0)

        psum = ps_ref[...]
        p_hi = psum.astype(BF16)
        r1 = psum - p_hi.astype(F32)
        p_mid = r1.astype(BF16)
        p_lo = (r1 - p_mid.astype(F32)).astype(BF16)
        n_blk = min(n_sel, -(-(q0 + tq) // (SEL_LEN * RANK_ROWS)) * RANK_ROWS)
        ov = ov_ref[:n_blk, :]
        imp = _dot_nt(ov, p_hi) + _dot_nt(ov, p_mid) + _dot_nt(ov, p_lo)
        blk = lax.broadcasted_iota(jnp.int32, (n_blk, tq), 0)
        cur = (q0 + lax.broadcasted_iota(jnp.int32, (n_blk, tq), 1)) // SEL_LEN
        valid = blk <= cur
        forced = jnp.logical_and(valid, jnp.logical_or(blk == 0, blk > cur - N_LOCAL_BLOCKS))
        val = jnp.where(forced, -NEG, jnp.where(valid, imp, NEG))
        rank = jnp.zeros((n_blk, tq), F32)
        for j in range(n_blk):
            rj = val[j:j + 1, :]
            ahead = jnp.logical_or(rj > val, jnp.logical_and(rj == val, blk > j))
            rank = rank + jnp.where(ahead, 1.0, 0.0)
        feat = jnp.where(rank < float(min(SEL_TOP_N, n_sel)), 0.0, NEG)
        feat = jnp.concatenate([jnp.zeros((SEL_F0, tq), F32), feat,
                                jnp.zeros((LANES - SEL_F0 - n_blk, tq), F32)], axis=0)
        feat_t = feat.T
        for slot in range(hg):
            qa_ref[slot * tq:(slot + 1) * tq, :] = (q_ref[0, slot, q0:q0 + tq, :].astype(F32) + feat_t).astype(BF16)

        for i, p in enumerate(range(qt, -1, -1)):
            back = qt - p
            tile(qa_ref, *sel_kv, p * tq, back if back < 2 else None, sel_sets[i % 2], sm_ref, sa_ref,
                 first=i == 0)

        gates = gate_ref[0, 0, q0:q0 + tq, :]
        g_hi = gates.astype(BF16)
        g_lo = (gates - g_hi.astype(F32)).astype(BF16)
        first = lax.broadcasted_iota(jnp.int32, gates.shape, 1) < GATE_COLS
        gx = _dot(jnp.where(first, g_hi, g_lo), gx_ref[...])
        low = lax.broadcasted_iota(jnp.int32, (tq, LANES), 1) < HEAD_DIM
        n_pairs = hg // 2
        for j in range(n_pairs):
            even = slice(j * tq, (j + 1) * tq)
            odd = slice(half_rows + j * tq, half_rows + (j + 1) * tq)
            out = gx[:, j * LANES:(j + 1) * LANES] * (oc_ref[even, :] + oc_ref[odd, :])
            for br, a_ref in ((1, sa_ref), (2, wa_ref)):
                a_e = a_ref[even, :]
                a_o = a_ref[odd, :]
                num = jnp.where(low, a_e, a_o)
                den = pltpu.roll(jnp.where(low, a_o, a_e), HEAD_DIM, 1)
                col = (br * n_pairs + j) * LANES
                out = out + gx[:, col:col + LANES] * (num / jnp.where(den > 0.0, den, 1.0))
            o_ref[0, q0:q0 + tq, j * LANES:(j + 1) * LANES] = out.astype(BF16)

    for c in range(n_qt):
        step(c)


def _attention(q, kvc, kv6, bias_c, bias_d, gates, ov_t, gate_expand):
    bsz, _, seq, _ = q.shape
    tq = min(ATT_Q, seq)
    hg = HEADS_PER_GROUP
    rows = hg * tq
    n_cmp = kvc.shape[3]
    assert ov_t.shape[0] <= MAX_SEL_BLOCKS and n_cmp <= tq and tq % LANES == 0
    kv_spec = lambda slot: pl.BlockSpec(
        (1, 1, seq, LANES), lambda g, b, slot=slot: (slot * bsz + b, g, 0, 0))
    kv_flat = kv6.reshape(KV_SLOTS * bsz, N_KV_GROUPS, seq, LANES)
    cmp_spec = lambda slot: pl.BlockSpec(
        (1, 1, n_cmp, LANES), lambda g, b, slot=slot: (slot * bsz + b, g, 0, 0))
    cmp_flat = kvc.reshape(3 * bsz, N_KV_GROUPS, n_cmp, LANES)
    stats = [pltpu.VMEM((rows, LANES), F32)] * 4
    tile_set = [pltpu.VMEM((rows, tq), F32), pltpu.VMEM((rows, tq), BF16), pltpu.VMEM((rows, LANES), F32)]
    return pl.pallas_call(
        _attn_body,
        grid=(N_KV_GROUPS, bsz),
        in_specs=[
            pl.BlockSpec((1, hg, seq, LANES), lambda g, b: (b, g, 0, 0)),
            cmp_spec(0), cmp_spec(1), cmp_spec(2),
            kv_spec(0), kv_spec(1), kv_spec(2), kv_spec(3), kv_spec(4), kv_spec(5),
            pl.BlockSpec((seq // tq, hg, tq, n_cmp), lambda g, b: (0, g, 0, 0)),
            pl.BlockSpec((hg, 3, tq, tq), lambda g, b: (g, 0, 0, 0)),
            pl.BlockSpec((1, 1, seq, gates.shape[-1]), lambda g, b: (b, g, 0, 0)),
            _const_spec(ov_t.shape), _const_spec(gate_expand.shape),
        ],
        out_specs=pl.BlockSpec((1, seq, hg * HEAD_DIM), lambda g, b: (b, 0, g)),
        out_shape=jax.ShapeDtypeStruct((bsz, seq, N_HEADS * HEAD_DIM), BF16),
        scratch_shapes=[
            pltpu.VMEM((tq, n_cmp), F32), pltpu.VMEM((rows, n_cmp), BF16), pltpu.VMEM((rows, n_cmp), F32),
            pltpu.VMEM((rows, LANES), F32), pltpu.VMEM((rows, LANES), BF16),
        ] + stats + tile_set * 4,
        compiler_params=_cparams(2),
        name="nsa_attn",
    )(q, cmp_flat, cmp_flat, cmp_flat, *([kv_flat] * KV_SLOTS), bias_c, bias_d, gates, ov_t, gate_expand)


def _rel_bucket_np(dist):
    n = np.maximum(dist, 0)
    max_exact = REL_BUCKETS // 2
    nf = np.maximum(n, 1).astype(np.float32)
    large = max_exact + (np.log(nf / np.float32(max_exact)) / np.float32(math.log(REL_MAX_DIST / max_exact))
                         * np.float32(REL_BUCKETS - max_exact)).astype(np.int32)
    return np.where(n < max_exact, n, np.minimum(large, REL_BUCKETS - 1)).astype(np.int32)


def _bias_tables(rel_table, seq, tq, n_cmp_pad):
    tbl2 = rel_table * LOG2E
    far = REL_BUCKETS - 1
    c_hi = tbl2[far].astype(BF16)
    c_lo = (tbl2[far] - c_hi.astype(F32)).astype(BF16)
    c2 = c_hi.astype(F32) + c_lo.astype(F32)

    def lookup(buckets, visible, head_axis, offset):
        shape = [1] * buckets.ndim
        shape[head_axis] = N_HEADS
        b = jnp.asarray(np.where(visible, buckets, -1).astype(np.int8))
        out = jnp.full(shape, NEG, F32)
        for bucket in range(REL_BUCKETS):
            out = jnp.where(b == bucket, (tbl2[bucket] - offset).reshape(shape), out)
        return out

    q = np.arange(tq)[:, None]
    k = np.arange(tq)[None, :]
    d0 = q - k
    d1 = q - k + tq
    d2 = q - k + 2 * tq
    assert tq >= REL_MAX_DIST and 2 * tq == WINDOW
    buckets = np.stack([_rel_bucket_np(d0), _rel_bucket_np(d1), _rel_bucket_np(d2)])[None]
    visible = np.stack([d0 >= 0, np.ones((tq, tq), bool), d2 < WINDOW])[None]
    bias_d = lookup(buckets, visible, 0, c2)

    n_qt = seq // tq
    t = (np.arange(n_qt)[:, None, None, None] * tq + np.arange(tq)[None, None, :, None])
    i = np.arange(n_cmp_pad)[None, None, None, :]
    dc = t - (i * CMP_STRIDE + CMP_LEN - 1)
    n_cmp = seq // CMP_STRIDE - CMP_LEN // CMP_STRIDE + 1
    vis_c = np.logical_and(dc >= 0, i < n_cmp)
    bias_c = lookup(_rel_bucket_np(dc), vis_c, 1, 0.0)

    cfeat = jnp.zeros((N_HEADS, 1, LANES - HEAD_DIM), BF16)
    cfeat = cfeat.at[:, 0, BIAS_F0 - HEAD_DIM].set(c_hi).at[:, 0, BIAS_F0 - HEAD_DIM + 1].set(c_lo)
    return bias_c, bias_d, cfeat


def _overlap_t(n_cmp_pad, n_sel):
    n_cmp = n_cmp_pad - CMP_LEN // CMP_STRIDE + 1
    i = np.arange(n_cmp_pad)[None, :]
    j = np.arange(n_sel)[:, None]
    ov = (i * CMP_STRIDE < (j + 1) * SEL_LEN) & (i * CMP_STRIDE + CMP_LEN > j * SEL_LEN) & (i < n_cmp)
    return ov.astype(np.float32)


def _gate_expand():
    n_pairs = HEADS_PER_GROUP // 2
    ex = np.zeros((GATE_COLS, N_BRANCH * n_pairs * LANES), np.float32)
    for head in range(HEADS_PER_GROUP):
        pair, odd = divmod(head, 2)
        for br in range(N_BRANCH):
            col = (br * n_pairs + pair) * LANES + odd * HEAD_DIM
            ex[N_BRANCH * head + br, col:col + HEAD_DIM] = 1.0
    return np.concatenate([ex, ex], axis=0)


def kernel(x, rel_table, ln_g, ln_b, ffn_w1, ffn_w3, ffn_w2, sgu_w_in, sgu_ln_g, sgu_ln_b, sgu_w_s,
           sgu_b_s, sgu_w_out, kv_w, cmp_pe, cmp_w1, cmp_b1, cmp_w2, nsa_w_qg, nsa_w_o):
    bsz, seq, d = x.shape
    t = bsz * seq
    h = x.reshape(t, d)
    n_chunks = seq // CMP_STRIDE
    tq = min(ATT_Q, seq)

    def ffn(h, layer, half, norm, mixer=None):
        return _ffn(h, ffn_w1[layer, half].astype(BF16), ffn_w3[layer, half].astype(BF16),
                    ffn_w2[layer, half].astype(BF16), ln_g[layer, norm][None], ln_b[layer, norm][None], mixer)

    shared = None
    for layer in range(DEPTH):
        if layer == N_A_LAYERS:
            kv_cmp, kv6 = _kv_proj(h.reshape(bsz, seq, d), kv_w.astype(BF16))
            r = kv_cmp.reshape(bsz, n_chunks, CMP_STRIDE, 2, N_KV_GROUPS, HEAD_DIM)
            r = r.transpose(3, 0, 4, 1, 2, 5).reshape(2, bsz, N_KV_GROUPS, n_chunks, CMP_STRIDE * HEAD_DIM)
            zpad = jnp.zeros(cmp_w2.shape[1:], cmp_w2.dtype)
            w2_pad = jnp.stack([jnp.concatenate([cmp_w2[0], zpad], axis=1),
                                jnp.concatenate([cmp_w2[1], zpad], axis=1),
                                jnp.concatenate([zpad, cmp_w2[1]], axis=1)]).astype(BF16)
            kvc = _compress(r, cmp_pe.reshape(2, 1, CMP_LEN * HEAD_DIM), cmp_w1.astype(BF16),
                            cmp_b1[:, None, :], w2_pad)
            bias_c, bias_d, cfeat = _bias_tables(rel_table, seq, tq, n_chunks)
            ov_t = jnp.asarray(_overlap_t(n_chunks, seq // SEL_LEN), BF16)
            shared = (kvc, kv6, bias_c, bias_d, cfeat, ov_t, jnp.asarray(_gate_expand(), BF16))
        h = ffn(h, layer, 0, 0)
        mixer = None
        if layer < N_A_LAYERS:
            a = layer
            h = _sgu(h, sgu_w_in[a].astype(BF16), sgu_ln_g[a][None], sgu_ln_b[a][None], sgu_w_s[a],
                     sgu_b_s[a].T, sgu_w_out[a].astype(BF16), ln_g[layer, 1][None], ln_b[layer, 1][None])
        else:
            bl = layer - N_A_LAYERS
            kvc, kv6, bias_c, bias_d, cfeat, ov_t, gate_expand = shared
            nq = N_HEADS * HEAD_DIM
            wq = nsa_w_qg[bl][:, :nq].astype(BF16)
            wg = nsa_w_qg[bl][:, nq:].reshape(d, N_KV_GROUPS, 3 * HEADS_PER_GROUP)
            wg = jnp.pad(wg, ((0, 0), (0, 0), (0, GATE_COLS - 3 * HEADS_PER_GROUP)))
            wg = wg.reshape(d, N_KV_GROUPS * GATE_COLS).astype(BF16)
            q, gates = _qg_proj(h.reshape(bsz, seq, d), wq, wg, cfeat)
            att = _attention(q, kvc, kv6, bias_c, bias_d, gates, ov_t, gate_expand)
            mixer = (att.reshape(t, nq), nsa_w_o[bl].astype(BF16), ln_g[layer, 1][None], ln_b[layer, 1][None])
        h = ffn(h, layer, 1, 2, mixer)
    return h.reshape(bsz, seq, d)
```
